```python
import math
import functools
import jax
import jax.numpy as jnp
from jax import lax
import numpy as np

D_MODEL = 1024
BATCH = 4
SEQ = 8192
DEPTH = 2
DEC_BATCH = 32
DEC_SEQ = 1
PAST_LEN = 16384
PAGE_SIZE = 128

RW_HEADS = 8
RW_HEAD = 64
RW_WIDTH = RW_HEADS * RW_HEAD
RW_LORA_W = 64
RW_LORA_A = 64
RW_LORA_G = 128
RW_GN_EPS = 64e-5
AT_HEADS = 8
AT_HEAD = 64
AT_WIDTH = AT_HEADS * AT_HEAD
IDX_HEADS = 8
IDX_HEAD = 64
TOPK_MAX = 256
Q_BLOCK = 128
N_BUCKETS = 32
MAX_DISTANCE = 128
MEM_LEN = 256
MEM_HEADS = 4
MEM_HEAD = 128
MEM_WIDTH = MEM_HEADS * MEM_HEAD
N_EXPERTS = 32
TOP_K = 4
D_FF = 1024
SWIGLU_LIMIT = 7.0
SWIGLU_ALPHA = 1.702
MOE_BLOCK_MAX = 512
N_BRANCH = 3
LN_EPS = 1e-5
DN_ALPHA = (2 * DEPTH) ** 0.25
DN_BETA = (8 * DEPTH) ** -0.25
RW_COLS = (RW_WIDTH, RW_WIDTH, RW_WIDTH, RW_LORA_W, RW_LORA_A, RW_LORA_G)
RW_IN = 3 * RW_WIDTH + RW_LORA_W + RW_LORA_A + RW_LORA_G
AT_COLS = (AT_WIDTH, AT_WIDTH, AT_WIDTH, IDX_HEADS * IDX_HEAD, IDX_HEAD, IDX_HEADS)
AT_IN = 3 * AT_WIDTH + IDX_HEADS * IDX_HEAD + IDX_HEAD + IDX_HEADS
MEM_IN = MEM_WIDTH
GATE_IN = N_BRANCH * D_MODEL
IN_WIDTH = RW_IN + AT_IN + MEM_IN + GATE_IN

kernel_name = 'rwkv7_dsa_memxattn_moe_deepnorm_step'


def _split(x, widths):
    outs, o = [], 0
    for w in widths:
        outs.append(x[..., o:o + w])
        o += w
    return outs


def layer_norm(x, g, b):
    xf = x.astype(jnp.float32)
    mu = jnp.mean(xf, -1, keepdims=True)
    var = jnp.mean(jnp.square(xf - mu), -1, keepdims=True)
    y = (xf - mu) * lax.rsqrt(var + LN_EPS) * g.astype(jnp.float32) + b.astype(jnp.float32)
    return y.astype(x.dtype)


def t5_bucket(rel):
    n = jnp.maximum(rel, 0)
    max_exact = N_BUCKETS // 2
    nf = jnp.maximum(n, 1).astype(jnp.float32)
    large = max_exact + (jnp.log(nf / max_exact) / math.log(MAX_DISTANCE / max_exact) * (N_BUCKETS - max_exact)).astype(jnp.int32)
    large = jnp.minimum(large, N_BUCKETS - 1)
    return jnp.where(n < max_exact, n, large)


def rwkv_branch(p_rw, p_prev, s0, lp):
    B, T, _ = p_rw.shape
    H, N = RW_HEADS, RW_HEAD
    xs = p_rw + (p_prev - p_rw) * lp['rw_mu']
    r, k, v, lw, la, lg = _split(xs, RW_COLS)
    w = -jax.nn.softplus(-(lp['rw_w0'] + jnp.tanh(lw) @ lp['rw_wB'])) - 0.5
    decay = jnp.exp(-jnp.exp(w.astype(jnp.float32)))
    a = jax.nn.sigmoid(lp['rw_a0'] + la @ lp['rw_aB'])
    g = jax.nn.sigmoid(lg) @ lp['rw_gB']
    heads = lambda t: t.reshape(B, T, H, N).astype(jnp.float32)
    kk = heads(k * lp['rw_kk'])
    kk = kk / jnp.maximum(jnp.linalg.norm(kk, axis=-1, keepdims=True), 1e-12)
    k = k * (1.0 + (a - 1.0) * lp['rw_ka'])
    r_h, k_h, v_h, a_h, w_h = heads(r), heads(k), heads(v), heads(a), heads(decay)
    b_h = kk * a_h

    def step(S, inp):
        r_t, w_t, k_t, v_t, kk_t, b_t = inp
        sa = jnp.einsum('bhvk,bhk->bhv', S, -kk_t)
        S = S * w_t[:, :, None, :] + sa[..., None] * b_t[:, :, None, :] + v_t[..., None] * k_t[:, :, None, :]
        y = jnp.einsum('bhvk,bhk->bhv', S, r_t)
        return S, y

    tm = lambda t: jnp.moveaxis(t, 1, 0)
    s_T, ys = lax.scan(step, s0.astype(jnp.float32), (tm(r_h), tm(w_h), tm(k_h), tm(v_h), tm(kk), tm(b_h)))
    y = jnp.moveaxis(ys, 0, 1)
    mu = jnp.mean(y, -1, keepdims=True)
    var = jnp.mean(jnp.square(y - mu), -1, keepdims=True)
    y = ((y - mu) * lax.rsqrt(var + RW_GN_EPS)).reshape(B, T, RW_WIDTH)
    y = y * lp['rw_gn_g'].astype(jnp.float32) + lp['rw_gn_b'].astype(jnp.float32)
    bonus = jnp.sum(r_h * k_h * lp['rw_rk'].reshape(H, N).astype(jnp.float32), -1, keepdims=True) * v_h
    out = (y + bonus.reshape(B, T, RW_WIDTH)) * g.astype(jnp.float32)
    return out.astype(p_rw.dtype), s_T.astype(s0.dtype)


def indexer_topk(iq, iw, ik, q_pos, topk):
    s = jnp.einsum('bqhd,bld->bqhl', iq, ik).astype(jnp.float32) * IDX_HEAD ** -0.5
    score = jnp.einsum('bqh,bqhl->bql', iw.astype(jnp.float32) * IDX_HEADS ** -0.5, jax.nn.relu(s))
    k_pos = jnp.arange(ik.shape[1])
    score = jnp.where(k_pos[None, None, :] <= q_pos[None, :, None], score, -jnp.inf)
    _, sel = lax.top_k(score, topk)
    return sel


def sparse_attend(q, kg, vg, sel, q_pos, rel_table):
    logits = jnp.einsum('bqhd,bqkhd->bqhk', q, kg).astype(jnp.float32) * AT_HEAD ** -0.5
    rel = q_pos[None, :, None] - sel
    bias = jnp.swapaxes(rel_table[t5_bucket(rel)], -1, -2)
    logits = logits + bias.astype(jnp.float32)
    logits = jnp.where((rel >= 0)[:, :, None, :], logits, -jnp.inf)
    p = jax.nn.softmax(logits, axis=-1)
    return jnp.einsum('bqhk,bqkhd->bqhd', p.astype(vg.dtype), vg)


def dsa_prompt(q, k, v, iq, ik, iw, rel_table):
    B, T = q.shape[:2]
    topk = min(TOPK_MAX, T // 4)
    nb = T // Q_BLOCK
    bidx = jnp.arange(B)[:, None, None]

    def block(args):
        qb, iqb, iwb, start = args
        q_pos = start + jnp.arange(Q_BLOCK)
        sel = indexer_topk(iqb, iwb, ik, q_pos, topk)
        return sparse_attend(qb, k[bidx, sel], v[bidx, sel], sel, q_pos, rel_table)

    to_blocks = lambda t: jnp.moveaxis(t.reshape(B, nb, Q_BLOCK, *t.shape[2:]), 1, 0)
    out = lax.map(block, (to_blocks(q), to_blocks(iq), to_blocks(iw), jnp.arange(nb) * Q_BLOCK))
    return jnp.moveaxis(out, 0, 1).reshape(B, T, AT_HEADS, AT_HEAD)


def dsa_sample(q, k_new, v_new, iq, ik_new, iw, layer, ck, cv, cik, page_table, rel_table):
    DB, Tn = q.shape[:2]
    L = PAST_LEN + Tn
    topk = min(TOPK_MAX, L // 4)
    ik_past = cik[layer, page_table].reshape(DB, PAST_LEN, IDX_HEAD)
    ik_all = jnp.concatenate([ik_past, ik_new], axis=1)
    q_pos = PAST_LEN + jnp.arange(Tn)
    sel = indexer_topk(iq, iw, ik_all, q_pos, topk)
    bidx = jnp.arange(DB)[:, None, None]
    past = jnp.minimum(sel, PAST_LEN - 1)
    phys = page_table[bidx, past // PAGE_SIZE]
    off = past % PAGE_SIZE
    new = jnp.clip(sel - PAST_LEN, 0, Tn - 1)
    is_new = (sel >= PAST_LEN)[..., None, None]
    kg = jnp.where(is_new, k_new[bidx, new], ck[layer, phys, off])
    vg = jnp.where(is_new, v_new[bidx, new], cv[layer, phys, off])
    return sparse_attend(q, kg, vg, sel, q_pos, rel_table)


def mem_attend(q, mk, mv):
    s = jnp.einsum('bthd,bmhd->bthm', q, mk).astype(jnp.float32) * MEM_HEAD ** -0.5
    p = jax.nn.softmax(s, axis=-1)
    return jnp.einsum('bthm,bmhd->bthd', p.astype(mv.dtype), mv)


def mix_block(x, lp, shift_row, s0, attn_fn, mem_k, mem_v):
    B, T, _ = x.shape
    proj = x @ lp['w_in']
    p_rw, p_at, p_mq, p_gate = _split(proj, (RW_IN, AT_IN, MEM_IN, GATE_IN))
    prev0 = (shift_row @ lp['w_in'][:, :RW_IN])[:, None, :]
    p_prev = jnp.concatenate([prev0, p_rw[:, :-1]], axis=1)
    y_rw, s_T = rwkv_branch(p_rw, p_prev, s0, lp)
    q, k, v, iq, ik, iw = _split(p_at, AT_COLS)
    q = q.reshape(B, T, AT_HEADS, AT_HEAD)
    k = k.reshape(B, T, AT_HEADS, AT_HEAD)
    v = v.reshape(B, T, AT_HEADS, AT_HEAD)
    iq = iq.reshape(B, T, IDX_HEADS, IDX_HEAD)
    y_at = attn_fn(q, k, v, iq, ik, iw).reshape(B, T, AT_WIDTH)
    y_mem = mem_attend(p_mq.reshape(B, T, MEM_HEADS, MEM_HEAD), mem_k, mem_v).reshape(B, T, MEM_WIDTH)
    g_rw, g_at, g_mem = _split(jax.nn.sigmoid(p_gate), (D_MODEL,) * N_BRANCH)
    merged = g_rw * (y_rw @ lp['w_br_rw']) + g_at * (y_at @ lp['w_br_at']) + g_mem * (y_mem @ lp['w_br_mem'])
    return merged @ lp['w_out'], s_T, k, v, ik


def moe_ffn(x, lp):
    B, T, D = x.shape
    n_tok = B * T
    xt = x.reshape(n_tok, D)
    logits = (xt @ lp['router_w'] + lp['router_b']).astype(jnp.float32)
    top_val, top_idx = lax.top_k(logits, TOP_K)
    gates = jax.nn.softmax(top_val, axis=-1)
    n_asg = n_tok * TOP_K
    avg = -(-n_asg // N_EXPERTS)
    blk = min(MOE_BLOCK_MAX, max(8, 1 << (avg - 1).bit_length()))
    n_blocks = -(-n_asg // blk) + N_EXPERTS
    cap = n_blocks * blk
    expert = top_idx.reshape(n_asg)
    token = jnp.repeat(jnp.arange(n_tok, dtype=jnp.int32), TOP_K)
    gate = gates.reshape(n_asg)
    order = jnp.argsort(expert)
    e_s, t_s, g_s = expert[order], token[order], gate[order]
    counts = jnp.bincount(expert, length=N_EXPERTS)
    starts = jnp.cumsum(counts) - counts
    padded = (counts + blk - 1) // blk * blk
    pends = jnp.cumsum(padded)
    pstarts = pends - padded
    dest = pstarts[e_s] + jnp.arange(n_asg) - starts[e_s]
    buf_tok = jnp.full((cap,), n_tok, jnp.int32).at[dest].set(t_s)
    buf_gate = jnp.zeros((cap,), jnp.float32).at[dest].set(g_s)
    blk_exp = jnp.minimum(jnp.searchsorted(pends, jnp.arange(n_blocks) * blk, side='right'), N_EXPERTS - 1)
    x_pad = jnp.concatenate([xt, jnp.zeros((1, D), xt.dtype)], axis=0)

    def run(args):
        tok_b, e = args
        xb = x_pad[tok_b]
        hg = jnp.minimum(xb @ lp['moe_w_gate'][e] + lp['moe_b_gate'][e], SWIGLU_LIMIT)
        hl = jnp.clip(xb @ lp['moe_w_up'][e] + lp['moe_b_up'][e], -SWIGLU_LIMIT, SWIGLU_LIMIT)
        h = hg * jax.nn.sigmoid(SWIGLU_ALPHA * hg) * (hl + 1.0)
        return h @ lp['moe_w_down'][e] + lp['moe_b_down'][e]

    out = lax.map(run, (buf_tok.reshape(n_blocks, blk), blk_exp))
    out = out.reshape(cap, D) * buf_gate[:, None].astype(out.dtype)
    y = jnp.zeros((n_tok + 1, D), out.dtype).at[buf_tok].add(out)[:n_tok]
    return y.reshape(B, T, D).astype(x.dtype)


def setup_inputs(seed: int = 0) -> dict:
    key = jax.random.key(seed)
    keys = jax.random.split(key, 64)
    ctr = iter(range(64))

    def nrm(shape, scale=1.0):
        return jax.random.normal(keys[next(ctr)], shape, jnp.float32) * scale

    n_pages = PAST_LEN // PAGE_SIZE
    n_used = DEC_BATCH * n_pages
    n_pool = n_used + n_used // 4
    perm = jax.random.permutation(keys[next(ctr)], n_pool)
    page_table = perm[:n_used].reshape(DEC_BATCH, n_pages).astype(jnp.int32)
    Dm, E, F = D_MODEL, N_EXPERTS, D_FF
    inp = {
        'x_prompt': nrm((BATCH, SEQ, Dm)),
        'x_sample': nrm((DEC_BATCH, DEC_SEQ, Dm)),
        'mem_prompt': nrm((BATCH, MEM_LEN, Dm)),
        'cache_k': nrm((DEPTH, n_pool, PAGE_SIZE, AT_HEADS, AT_HEAD)),
        'cache_v': nrm((DEPTH, n_pool, PAGE_SIZE, AT_HEADS, AT_HEAD)),
        'cache_idx_k': nrm((DEPTH, n_pool, PAGE_SIZE, IDX_HEAD)),
        'page_table': page_table,
        'cache_mem_k': nrm((DEPTH, DEC_BATCH, MEM_LEN, MEM_HEADS, MEM_HEAD)),
        'cache_mem_v': nrm((DEPTH, DEC_BATCH, MEM_LEN, MEM_HEADS, MEM_HEAD)),
        'state_rwkv': nrm((DEPTH, DEC_BATCH, RW_HEADS, RW_HEAD, RW_HEAD), 0.5),
        'state_shift': nrm((DEPTH, DEC_BATCH, Dm)),
        'w_in': nrm((DEPTH, Dm, IN_WIDTH), Dm ** -0.5),
        'rw_mu': jax.random.uniform(keys[next(ctr)], (DEPTH, RW_IN), jnp.float32),
        'rw_w0': nrm((DEPTH, RW_WIDTH), 0.5) - 1.0,
        'rw_wB': nrm((DEPTH, RW_LORA_W, RW_WIDTH), 0.1),
        'rw_a0': nrm((DEPTH, RW_WIDTH), 0.1),
        'rw_aB': nrm((DEPTH, RW_LORA_A, RW_WIDTH), 0.5 * RW_LORA_A ** -0.5),
        'rw_gB': nrm((DEPTH, RW_LORA_G, RW_WIDTH), RW_LORA_G ** -0.5),
        'rw_kk': 0.85 + nrm((DEPTH, RW_WIDTH), 0.05),
        'rw_ka': 1.0 + nrm((DEPTH, RW_WIDTH), 0.05),
        'rw_rk': nrm((DEPTH, RW_WIDTH), 0.1),
        'rw_gn_g': 1.0 + nrm((DEPTH, RW_WIDTH), 0.05),
        'rw_gn_b': nrm((DEPTH, RW_WIDTH), 0.02),
        'mem_wk': nrm((DEPTH, Dm, MEM_WIDTH), Dm ** -0.5),
        'mem_wv': nrm((DEPTH, Dm, MEM_WIDTH), Dm ** -0.5),
        'rel_bias': nrm((N_BUCKETS, AT_HEADS), 0.5),
        'w_br_rw': nrm((DEPTH, RW_WIDTH, Dm), DN_BETA * RW_WIDTH ** -0.5),
        'w_br_at': nrm((DEPTH, AT_WIDTH, Dm), DN_BETA * AT_WIDTH ** -0.5),
        'w_br_mem': nrm((DEPTH, MEM_WIDTH, Dm), DN_BETA * MEM_WIDTH ** -0.5),
        'w_out': nrm((DEPTH, Dm, Dm), DN_BETA * Dm ** -0.5),
        'ln1_g': 1.0 + nrm((DEPTH, Dm), 0.05),
        'ln1_b': nrm((DEPTH, Dm), 0.02),
        'ln2_g': 1.0 + nrm((DEPTH, Dm), 0.05),
        'ln2_b': nrm((DEPTH, Dm), 0.02),
        'router_w': nrm((DEPTH, Dm, E), Dm ** -0.5),
        'router_b': nrm((DEPTH, E), 0.01),
        'moe_w_gate': nrm((DEPTH, E, Dm, F), Dm ** -0.5),
        'moe_b_gate': nrm((DEPTH, E, F), 0.01),
        'moe_w_up': nrm((DEPTH, E, Dm, F), Dm ** -0.5),
        'moe_b_up': nrm((DEPTH, E, F), 0.01),
        'moe_w_down': nrm((DEPTH, E, F, Dm), DN_BETA * F ** -0.5),
        'moe_b_down': nrm((DEPTH, E, Dm), 0.01),
    }
    return inp


def reference(x_prompt, x_sample, mem_prompt, cache_k, cache_v, cache_idx_k, page_table, cache_mem_k, cache_mem_v, state_rwkv, state_shift, w_in, rw_mu, rw_w0, rw_wB, rw_a0, rw_aB, rw_gB, rw_kk, rw_ka, rw_rk, rw_gn_g, rw_gn_b, mem_wk, mem_wv, rel_bias, w_br_rw, w_br_at, w_br_mem, w_out, ln1_g, ln1_b, ln2_g, ln2_b, router_w, router_b, moe_w_gate, moe_b_gate, moe_w_up, moe_b_up, moe_w_down, moe_b_down):
    xp, xs = x_prompt, x_sample
    bp, bs = xp.shape[0], xs.shape[0]
    kp_l, vp_l, ikp_l, mkp_l, mvp_l, sp_l, shp_l = [], [], [], [], [], [], []
    ks_l, vs_l, iks_l, ss_l, shs_l = [], [], [], [], []
    for l in range(DEPTH):
        lp = {
            'w_in': w_in[l], 'rw_mu': rw_mu[l], 'rw_w0': rw_w0[l], 'rw_wB': rw_wB[l],
            'rw_a0': rw_a0[l], 'rw_aB': rw_aB[l], 'rw_gB': rw_gB[l], 'rw_kk': rw_kk[l],
            'rw_ka': rw_ka[l], 'rw_rk': rw_rk[l], 'rw_gn_g': rw_gn_g[l], 'rw_gn_b': rw_gn_b[l],
            'w_br_rw': w_br_rw[l], 'w_br_at': w_br_at[l], 'w_br_mem': w_br_mem[l], 'w_out': w_out[l],
            'router_w': router_w[l], 'router_b': router_b[l],
            'moe_w_gate': moe_w_gate[l], 'moe_b_gate': moe_b_gate[l],
            'moe_w_up': moe_w_up[l], 'moe_b_up': moe_b_up[l],
            'moe_w_down': moe_w_down[l], 'moe_b_down': moe_b_down[l],
        }
        mk = (mem_prompt @ mem_wk[l]).reshape(bp, MEM_LEN, MEM_HEADS, MEM_HEAD)
        mv = (mem_prompt @ mem_wv[l]).reshape(bp, MEM_LEN, MEM_HEADS, MEM_HEAD)
        attn_p = functools.partial(dsa_prompt, rel_table=rel_bias)
        shp_l.append(xp[:, -1])
        mix, s_new, k_new, v_new, ik_new = mix_block(
            xp, lp, jnp.zeros((bp, D_MODEL), xp.dtype),
            jnp.zeros((bp, RW_HEADS, RW_HEAD, RW_HEAD), jnp.float32), attn_p, mk, mv)
        kp_l.append(k_new); vp_l.append(v_new); ikp_l.append(ik_new)
        mkp_l.append(mk); mvp_l.append(mv); sp_l.append(s_new)
        xp = layer_norm(DN_ALPHA * xp + mix, ln1_g[l], ln1_b[l])
        xp = layer_norm(DN_ALPHA * xp + moe_ffn(xp, lp), ln2_g[l], ln2_b[l])
        attn_s = functools.partial(dsa_sample, layer=l, ck=cache_k, cv=cache_v, cik=cache_idx_k,
                                   page_table=page_table, rel_table=rel_bias)
        shs_l.append(xs[:, -1])
        mix, s_new, k_new, v_new, ik_new = mix_block(
            xs, lp, state_shift[l], state_rwkv[l], attn_s, cache_mem_k[l], cache_mem_v[l])
        ks_l.append(k_new); vs_l.append(v_new); iks_l.append(ik_new); ss_l.append(s_new)
        xs = layer_norm(DN_ALPHA * xs + mix, ln1_g[l], ln1_b[l])
        xs = layer_norm(DN_ALPHA * xs + moe_ffn(xs, lp), ln2_g[l], ln2_b[l])
    return (xp, xs,
            jnp.stack(kp_l), jnp.stack(vp_l), jnp.stack(ikp_l), jnp.stack(mkp_l), jnp.stack(mvp_l),
            jnp.stack(sp_l), jnp.stack(shp_l),
            jnp.stack(ks_l), jnp.stack(vs_l), jnp.stack(iks_l), jnp.stack(ss_l), jnp.stack(shs_l))
```

```python
import functools
import math

import numpy as np
import jax
import jax.numpy as jnp
from jax import lax
from jax.experimental import pallas as pl
from jax.experimental.pallas import tpu as pltpu

F32 = jnp.float32
BF16 = jnp.bfloat16
I32 = jnp.int32

D_MODEL = 1024
RW_HEADS, RW_HEAD, RW_WIDTH = 8, 64, 512
RW_LORA_W, RW_LORA_A, RW_LORA_G = 64, 64, 128
RW_IN = 3 * RW_WIDTH + RW_LORA_W + RW_LORA_A + RW_LORA_G
RW_GN_EPS = 64e-5
AT_HEADS, AT_HEAD, AT_WIDTH = 8, 64, 512
IDX_HEADS, IDX_HEAD = 8, 64
AT_IN = 3 * AT_WIDTH + IDX_HEADS * IDX_HEAD + IDX_HEAD + IDX_HEADS
TOPK_MAX = 256
N_BUCKETS, MAX_DISTANCE = 32, 128
MEM_HEADS, MEM_HEAD, MEM_WIDTH = 4, 128, 512
N_EXPERTS, TOP_K, D_FF = 32, 4, 1024
SWIGLU_LIMIT, SWIGLU_ALPHA = 7.0, 1.702
LN_EPS = 1e-5
PAGE = 128

LANES = 128
CHUNK = 64
QBLK = 128
KCH = 512
NEG = -1e30
INT_MIN = -(2 ** 31)
INT_MAX = 2 ** 31 - 1
VMEM_LIMIT = 56 * 1024 * 1024


def _cp(n_axes, vmem=VMEM_LIMIT):
    return pltpu.CompilerParams(dimension_semantics=("arbitrary",) * n_axes,
                                vmem_limit_bytes=vmem)


def _dot(a, b):
    return jnp.dot(a.astype(BF16), b.astype(BF16), preferred_element_type=F32)


def _dot_nt(a, b):
    return lax.dot_general(a.astype(BF16), b.astype(BF16), (((1,), (1,)), ((), ())),
                           preferred_element_type=F32)


def _dot_tn(a, b):
    return lax.dot_general(a.astype(BF16), b.astype(BF16), (((0,), (0,)), ((), ())),
                           preferred_element_type=F32)


def _split2(a):
    hi = a.astype(BF16)
    lo = (a - hi.astype(F32)).astype(BF16)
    return hi, lo


def _split3(a):
    hi = a.astype(BF16)
    r1 = a - hi.astype(F32)
    mid = r1.astype(BF16)
    lo = (r1 - mid.astype(F32)).astype(BF16)
    return hi, mid, lo


def _dot_x(a, b_exact):
    hi, mid, lo = _split3(a)
    d = lambda x: jnp.dot(x, b_exact, preferred_element_type=F32)
    return d(hi) + d(mid) + d(lo)


def _xdot(b_exact, a):
    hi, mid, lo = _split3(a)
    d = lambda x: jnp.dot(b_exact, x, preferred_element_type=F32)
    return d(hi) + d(mid) + d(lo)


def _dot_hp(a, b):
    ah, al = _split2(a)
    bh, bl = _split2(b)
    d = lambda x, y: jnp.dot(x, y, preferred_element_type=F32)
    return d(ah, bh) + d(ah, bl) + d(al, bh)


def _layer_norm(z, g, b):
    mu = jnp.mean(z, axis=-1, keepdims=True)
    zc = z - mu
    var = jnp.mean(zc * zc, axis=-1, keepdims=True)
    return zc * lax.rsqrt(var + LN_EPS) * g + b


def _head_mask(hh):
    lane = lax.broadcasted_iota(I32, (1, LANES), 1)
    return (lane >= hh * 64) & (lane < (hh + 1) * 64)


def _proj_kernel(x_ref, w_ref, *o_refs, cols):
    acc = jnp.dot(x_ref[...].astype(BF16), w_ref[...], preferred_element_type=F32)
    for o_ref, (c0, wd, scale) in zip(o_refs, cols):
        piece = acc[:, c0:c0 + wd]
        if scale != 1.0:
            piece = piece * scale
        o_ref[...] = piece.astype(o_ref.dtype)


def _proj(x, w_bf, cols, dtypes, tm):
    n, kdim = x.shape
    m = w_bf.shape[1]
    tm = min(tm, n)
    grid = (n // tm,)
    return pl.pallas_call(
        functools.partial(_proj_kernel, cols=tuple(cols)),
        grid=grid,
        in_specs=[pl.BlockSpec((tm, kdim), lambda i: (i, 0)),
                  pl.BlockSpec((kdim, m), lambda i: (0, 0))],
        out_specs=[pl.BlockSpec((tm, wd), lambda i: (i, 0)) for (_, wd, _) in cols],
        out_shape=[jax.ShapeDtypeStruct((n, wd), dt) for (_, wd, _), dt in zip(cols, dtypes)],
        compiler_params=_cp(1),
        name="proj",
    )(x, w_bf)


def _rw_prep(p, p_prev, mu, w0, wB2, a0, aB2, gB, kkp, ka, segm):
    xs = p + (p_prev - p) * mu
    r = xs[:, 0:512]
    k = xs[:, 512:1024]
    v = xs[:, 1024:1536]
    lwla = xs[:, 1536:1664]
    lg = xs[:, 1664:1792]
    w = -jax.nn.softplus(-(w0 + _dot(jnp.tanh(lwla), wB2))) - 0.5
    ld = -jnp.exp(w)
    a = jax.nn.sigmoid(a0 + _dot(lwla, aB2))
    g = _dot(jax.nn.sigmoid(lg), gB)
    kkr = k * kkp
    ss = _dot_x(kkr * kkr, segm)
    kk = kkr / jnp.maximum(jnp.sqrt(ss), 1e-12)
    k2 = k * (1.0 + (a - 1.0) * ka)
    b = kk * a
    return r, k2, v, kk, b, ld, g


def _rw_chunk_kernel(p_ref, pb_ref, p0_ref, mu_ref, w0_ref, wB_ref, a0_ref, aB_ref, gB_ref,
                     kk_ref, ka_ref, rk_ref, segm_ref, ltri_ref,
                     y1_ref, y2_ref, bonus_ref, g_ref, ac_ref, cc_ref,
                     r_s, k_s, v_s, kk_s, b_s, ld_s, *, tm):
    i = pl.program_id(1)
    p = p_ref[...]
    prev_last = jnp.where(i == 0, p0_ref[...], pb_ref[7:8, :])
    rolled = pltpu.roll(p, 1, axis=0)
    row = lax.broadcasted_iota(I32, p.shape, 0)
    p_prev = jnp.where(row == 0, prev_last, rolled)
    segm = segm_ref[...]
    r, k2, v, kk, b, ld, g = _rw_prep(p, p_prev, mu_ref[...], w0_ref[...], wB_ref[...], a0_ref[...],
                                      aB_ref[...], gB_ref[...], kk_ref[...], ka_ref[...], segm)
    g_ref[...] = g
    bonus_ref[...] = _dot_x(r * k2 * rk_ref[...], segm) * v
    r_s[...] = r
    k_s[...] = k2
    v_s[...] = v
    kk_s[...] = kk
    b_s[...] = b
    ld_s[...] = ld

    ltri = ltri_ref[...]
    ri = lax.broadcasted_iota(I32, (CHUNK, CHUNK), 0)
    ci = lax.broadcasted_iota(I32, (CHUNK, CHUNK), 1)
    strict = ri > ci
    incl = ri >= ci
    eye = (lax.broadcasted_iota(I32, (LANES, LANES), 0) == lax.broadcasted_iota(I32, (LANES, LANES), 1))

    def chunk(c, carry):
        rows = pl.ds(pl.multiple_of(c * CHUNK, CHUNK), CHUNK)
        ldc = ld_s[rows, :]
        cum = _xdot(ltri, ldc)
        cum_c = cum[CHUNK - 1:CHUNK, :]
        pin = jnp.exp(cum)
        pex = jnp.exp(cum - ldc)
        pinv = jnp.exp(-cum)
        pend = jnp.exp(cum_c - cum)
        pc = jnp.exp(cum_c)
        kkc, bc, kc, rc, vc = kk_s[rows, :], b_s[rows, :], k_s[rows, :], r_s[rows, :], v_s[rows, :]
        at = -kkc * pex
        rt = rc * pin
        bt = bc * pinv
        kt = kc * pinv
        bend = bc * pend
        kend = kc * pend
        for pr in range(RW_HEADS // 2):
            ln = slice(pr * LANES, (pr + 1) * LANES)
            atp, rtp, btp, ktp, vp = at[:, ln], rt[:, ln], bt[:, ln], kt[:, ln], vc[:, ln]
            ar = jnp.concatenate([atp, rtp], axis=0)
            bkend = jnp.concatenate([bend[:, ln], kend[:, ln]], axis=0)
            zv = jnp.concatenate([jnp.zeros_like(vp), vp], axis=1)
            acp = jnp.where(eye, pc[:, ln], 0.0)
            ccp = jnp.zeros((LANES, LANES), F32)
            y1p = jnp.zeros((CHUNK, LANES), F32)
            y2p = jnp.zeros((CHUNK, LANES), F32)
            for hh in range(2):
                hm = _head_mask(hh)
                arm = jnp.where(hm, ar, 0.0)
                gb = _dot_nt(arm, btp)
                gk = _dot_nt(arm, ktp)
                nm = jnp.where(strict, gb[:CHUNK], 0.0)
                lrb = jnp.where(incl, gb[CHUNK:], 0.0)
                aak = jnp.where(strict, gk[:CHUNK], 0.0)
                lrk = jnp.where(incl, gk[CHUNK:], 0.0)
                x = jnp.concatenate([atp, _dot(aak, vp)], axis=1)
                n2 = _dot(nm, nm)
                n4 = _dot(n2, n2)
                n8 = _dot(n4, n4)
                n16 = _dot(n8, n8)
                n32 = _dot(n16, n16)
                for q in (n32, n16, n8, n4, n2, nm):
                    x = x + _dot(q, x)
                lhs = jnp.where(hm, bkend, 0.0)
                rhs = jnp.concatenate([x, zv], axis=0)
                res = _dot_tn(lhs, rhs)
                acp = acp + jnp.where(hm, res[:, :LANES], 0.0)
                ccp = ccp + jnp.where(hm, res[:, LANES:], 0.0)
                lx = _dot(lrb, x)
                y1p = y1p + jnp.where(hm, rtp + lx[:, :LANES], 0.0)
                y2p = y2p + jnp.where(hm, lx[:, LANES:] + _dot(lrk, vp), 0.0)
            y1_ref[rows, ln] = y1p
            y2_ref[rows, ln] = y2p
            ac_ref[c, :, ln] = acp
            cc_ref[c, :, ln] = ccp
        return carry

    lax.fori_loop(0, tm // CHUNK, chunk, 0)


def _rw_chunks(p_rw, prev0, lw, batch, seq, tm):
    n = batch * seq
    nt = seq // tm
    ncs = tm // CHUNK
    full = lambda shape: pl.BlockSpec(shape, lambda b, i: (0,) * len(shape))
    row_blk = pl.BlockSpec((tm, RW_WIDTH), lambda b, i: (b * nt + i, 0))
    mat_blk = pl.BlockSpec((ncs, LANES, RW_WIDTH), lambda b, i: (b * nt + i, 0, 0))
    return pl.pallas_call(
        functools.partial(_rw_chunk_kernel, tm=tm),
        grid=(batch, nt),
        in_specs=[pl.BlockSpec((tm, RW_IN), lambda b, i: (b * nt + i, 0)),
                  pl.BlockSpec((8, RW_IN), lambda b, i: (jnp.maximum((b * nt + i) * (tm // 8) - 1, 0), 0)),
                  pl.BlockSpec((None, 1, RW_IN), lambda b, i: (b, 0, 0)),
                  full((1, RW_IN)), full((1, RW_WIDTH)), full((LANES, RW_WIDTH)), full((1, RW_WIDTH)),
                  full((LANES, RW_WIDTH)), full((LANES, RW_WIDTH)), full((1, RW_WIDTH)), full((1, RW_WIDTH)),
                  full((1, RW_WIDTH)), full((RW_WIDTH, RW_WIDTH)), full((CHUNK, CHUNK))],
        out_specs=[row_blk, row_blk, row_blk, row_blk, mat_blk, mat_blk],
        out_shape=[jax.ShapeDtypeStruct((n, RW_WIDTH), F32)] * 4
                  + [jax.ShapeDtypeStruct((n // CHUNK, LANES, RW_WIDTH), F32)] * 2,
        scratch_shapes=[pltpu.VMEM((tm, RW_WIDTH), F32)] * 6,
        compiler_params=_cp(2),
        name="rwkv_chunks",
    )(p_rw, p_rw, prev0, lw["mu"], lw["w0"], lw["wB2"], lw["a0"], lw["aB2"], lw["gB"],
      lw["kk"], lw["ka"], lw["rk"], lw["segm"], lw["ltri"])


def _rw_scan_kernel(y1_ref, y2_ref, bonus_ref, g_ref, ac_ref, cc_ref, h0_ref, gng_ref, gnb_ref, segm_ref,
                    y_ref, hfin_ref, h_s, *, ncs):
    i = pl.program_id(1)

    @pl.when(i == 0)
    def _():
        h_s[...] = h0_ref[...]

    segm = segm_ref[...]
    for c in range(ncs):
        rows = slice(c * CHUNK, (c + 1) * CHUNK)
        parts = []
        for pr in range(RW_HEADS // 2):
            ln = slice(pr * LANES, (pr + 1) * LANES)
            hp = h_s[:, ln]
            parts.append(_dot_hp(y1_ref[rows, ln], hp) + y2_ref[rows, ln])
            h_s[:, ln] = _dot_hp(ac_ref[c, :, ln], hp) + cc_ref[c, :, ln]
        y = jnp.concatenate(parts, axis=1)
        mu = _dot_x(y, segm) * (1.0 / RW_HEAD)
        yc = y - mu
        var = _dot_x(yc * yc, segm) * (1.0 / RW_HEAD)
        yn = yc * lax.rsqrt(var + RW_GN_EPS) * gng_ref[...] + gnb_ref[...]
        y_ref[rows, :] = ((yn + bonus_ref[rows, :]) * g_ref[rows, :]).astype(y_ref.dtype)
    hfin_ref[...] = h_s[...]


def _rw_scan(y1, y2, bonus, g, ac, cc, h0, lw, batch, seq, tm):
    n = batch * seq
    nt = seq // tm
    ncs = tm // CHUNK
    full = lambda shape: pl.BlockSpec(shape, lambda b, i: (0,) * len(shape))
    row_blk = pl.BlockSpec((tm, RW_WIDTH), lambda b, i: (b * nt + i, 0))
    mat_blk = pl.BlockSpec((ncs, LANES, RW_WIDTH), lambda b, i: (b * nt + i, 0, 0))
    st_blk = pl.BlockSpec((None, LANES, RW_WIDTH), lambda b, i: (b, 0, 0))
    return pl.pallas_call(
        functools.partial(_rw_scan_kernel, ncs=ncs),
        grid=(batch, nt),
        in_specs=[row_blk, row_blk, row_blk, row_blk, mat_blk, mat_blk, st_blk,
                  full((1, RW_WIDTH)), full((1, RW_WIDTH)), full((RW_WIDTH, RW_WIDTH))],
        out_specs=[row_blk, st_blk],
        out_shape=[jax.ShapeDtypeStruct((n, RW_WIDTH), BF16),
                   jax.ShapeDtypeStruct((batch, LANES, RW_WIDTH), F32)],
        scratch_shapes=[pltpu.VMEM((LANES, RW_WIDTH), F32)],
        compiler_params=_cp(2),
        name="rwkv_scan",
    )(y1, y2, bonus, g, ac, cc, h0, lw["gn_g"], lw["gn_b"], lw["segm"])


def _rw_step_prep_kernel(p_ref, pp_ref, mu_ref, w0_ref, wB_ref, a0_ref, aB_ref, gB_ref,
                         kk_ref, ka_ref, segm_ref, r_o, k_o, v_o, kk_o, b_o, w_o, g_o):
    r, k2, v, kk, b, ld, g = _rw_prep(p_ref[...], pp_ref[...], mu_ref[...], w0_ref[...], wB_ref[...],
                                      a0_ref[...], aB_ref[...], gB_ref[...], kk_ref[...], ka_ref[...],
                                      segm_ref[...])
    r_o[...] = r
    k_o[...] = k2
    v_o[...] = v
    kk_o[...] = kk
    b_o[...] = b
    w_o[...] = jnp.exp(ld)
    g_o[...] = g


def _rw_step_prep(p, p_prev, lw):
    n = p.shape[0]
    return pl.pallas_call(
        _rw_step_prep_kernel,
        out_shape=[jax.ShapeDtypeStruct((n, RW_WIDTH), F32)] * 7,
        compiler_params=pltpu.CompilerParams(vmem_limit_bytes=VMEM_LIMIT),
        name="rwkv_step_prep",
    )(p, p_prev, lw["mu"], lw["w0"], lw["wB2"], lw["a0"], lw["aB2"], lw["gB"], lw["kk"], lw["ka"], lw["segm"])


def _rw_step_kernel(s_ref, r_ref, w_ref, k_ref, kk_ref, b_ref, v_ref, g_ref, rk_ref, gng_ref, gnb_ref,
                    s_o, y_o):
    s = s_ref[...]
    r, w, k, kk, b = r_ref[...], w_ref[...], k_ref[...], kk_ref[...], b_ref[...]
    v = v_ref[...]
    rb = lambda a: a.astype(BF16).astype(F32)
    sa = jnp.sum(rb(s) * rb(-kk), axis=-1, keepdims=True)
    sn = s * w + sa * b + v * k
    s_o[...] = sn
    y = jnp.sum(rb(sn) * rb(r), axis=-1, keepdims=True)
    mu = jnp.mean(y, axis=1, keepdims=True)
    yc = y - mu
    var = jnp.mean(yc * yc, axis=1, keepdims=True)
    yn = yc * lax.rsqrt(var + RW_GN_EPS) * gng_ref[...] + gnb_ref[...]
    bonus = jnp.sum(r * k * rk_ref[...], axis=-1, keepdims=True) * v
    y_o[...] = (yn + bonus) * g_ref[...]


def _rw_step(s0, r, w, k, kk, b, v, g, lw):
    db = s0.shape[0]
    hk = lambda a: a.reshape(db, RW_HEADS, 1, RW_HEAD)
    hv = lambda a: a.reshape(db, RW_HEADS, RW_HEAD, 1)
    rowk = pl.BlockSpec((None, RW_HEADS, 1, RW_HEAD), lambda i: (i, 0, 0, 0))
    colv = pl.BlockSpec((None, RW_HEADS, RW_HEAD, 1), lambda i: (i, 0, 0, 0))
    st = pl.BlockSpec((None, RW_HEADS, RW_HEAD, RW_HEAD), lambda i: (i, 0, 0, 0))
    prk = pl.BlockSpec((RW_HEADS, 1, RW_HEAD), lambda i: (0, 0, 0))
    pcv = pl.BlockSpec((RW_HEADS, RW_HEAD, 1), lambda i: (0, 0, 0))
    s_new, y = pl.pallas_call(
        _rw_step_kernel,
        grid=(db,),
        in_specs=[st, rowk, rowk, rowk, rowk, rowk, colv, colv, prk, pcv, pcv],
        out_specs=[st, colv],
        out_shape=[jax.ShapeDtypeStruct(s0.shape, F32),
                   jax.ShapeDtypeStruct((db, RW_HEADS, RW_HEAD, 1), F32)],
        compiler_params=_cp(1),
        name="rwkv_step",
    )(s0, hk(r), hk(w), hk(k), hk(kk), hk(b), hv(v), hv(g),
      lw["rk"].reshape(RW_HEADS, 1, RW_HEAD), lw["gn_g"].reshape(RW_HEADS, RW_HEAD, 1),
      lw["gn_b"].reshape(RW_HEADS, RW_HEAD, 1))
    return s_new, y.reshape(db, RW_WIDTH)


def _sortable(x):
    bits = pltpu.bitcast(x, I32)
    return jnp.where(bits < 0, bits ^ jnp.int32(INT_MAX), bits)


def _dsa_prompt_kernel(iq_ref, iw_ref, q_ref, ik_ref, k_ref, v_ref, bias_ref, o_ref,
                       ksc, mbsc, wbsc, iqm, tsc, jsc, *, topk):
    j = pl.program_id(1)
    nk = j + 1
    ones_bf = jnp.ones((LANES, LANES), BF16)
    row = lax.broadcasted_iota(I32, (QBLK, LANES), 0)
    col = lax.broadcasted_iota(I32, (QBLK, LANES), 1)
    qpos = j * QBLK + row
    wscale = IDX_HEADS ** -0.5 * IDX_HEAD ** -0.5

    iw = iw_ref[...]
    for h in range(IDX_HEADS):
        wbsc[h] = jnp.broadcast_to(iw[:, h:h + 1] * wscale, (QBLK, LANES))
        iqp = iq_ref[:, (h // 2) * LANES:(h // 2 + 1) * LANES]
        iqm[h] = jnp.where(_head_mask(h % 2), iqp, jnp.zeros_like(iqp))

    def score_chunk(c, carry):
        c0 = pl.multiple_of(c * KCH, KCH)
        ikc = ik_ref[pl.ds(c0, KCH), :]
        acc = jnp.zeros((QBLK, KCH), F32)
        for h in range(IDX_HEADS):
            s = _dot_nt(iqm[h], ikc)
            wb = wbsc[h]
            acc = acc + jnp.maximum(s, 0.0) * jnp.concatenate([wb] * (KCH // LANES), axis=1)
        key = _sortable(acc)
        for t in range(KCH // LANES):
            kb = c * (KCH // LANES) + t
            valid = (kb * LANES + col) <= qpos
            ksc[kb] = jnp.where(valid, key[:, t * LANES:(t + 1) * LANES], jnp.int32(INT_MIN))
        return carry

    lax.fori_loop(0, j // (KCH // LANES) + 1, score_chunk, 0)

    def lane_total(cnt):
        return jnp.dot(cnt.astype(BF16), ones_bf, preferred_element_type=F32)

    def count(pred):
        def body(kb, cnt):
            return cnt + pred(kb, ksc[kb]).astype(I32)
        return lane_total(lax.fori_loop(0, nk, body, jnp.zeros((QBLK, LANES), I32)))

    ktop = jnp.float32(topk)

    def bisect(_, lohi):
        lo, hi = lohi
        mid = lo + lax.shift_right_logical(hi - lo, 1)
        ok = count(lambda kb, key: key >= mid) >= ktop
        return jnp.where(ok, mid, lo), jnp.where(ok, hi, mid)

    lo, _ = lax.fori_loop(0, 32, bisect, (jnp.full((QBLK, LANES), INT_MIN, I32),
                                           jnp.full((QBLK, LANES), INT_MAX, I32)))
    thr = lo
    tsc[...] = thr
    jsc[...] = jnp.full((QBLK, LANES), INT_MAX, I32)
    c_ge = count(lambda kb, key: key >= thr)

    @pl.when(jnp.max(c_ge) > ktop)
    def _ties():
        need = ktop - count(lambda kb, key: key > thr)

        def bis_j(_, lohi):
            lo_j, hi_j = lohi
            mid = lax.shift_right_arithmetic(lo_j + hi_j, 1)
            ok = count(lambda kb, key: (key == thr) & ((kb * LANES + col) <= mid)) >= need
            return jnp.where(ok, lo_j, mid), jnp.where(ok, mid, hi_j)

        _, hi_j = lax.fori_loop(0, 15, bis_j, (jnp.full((QBLK, LANES), -1, I32),
                                               jnp.full((QBLK, LANES), 1, I32) * (nk * LANES - 1)))
        jsc[...] = hi_j

    thr = tsc[...]
    jlim = jsc[...]

    def mask_block(kb, carry):
        key = ksc[kb]
        lpos = kb * LANES + col
        sel = (key > thr) | ((key == thr) & (lpos <= jlim))
        mbsc[kb] = jnp.where(sel & (lpos <= qpos), 0.0, NEG)
        return carry

    lax.fori_loop(0, nk, mask_block, 0)

    nfull = jnp.maximum(j - 1, 0) // (KCH // LANES)
    outs = []
    for pr in range(AT_HEADS // 2):
        ln = slice(pr * LANES, (pr + 1) * LANES)
        qp = q_ref[:, ln]
        o_pair = jnp.zeros((QBLK, LANES), F32)
        for hh in range(2):
            h = 2 * pr + hh
            hm = _head_mask(hh)
            qm = jnp.where(hm, qp, jnp.zeros_like(qp))

            def update(carry, s, vc):
                m, l, acc = carry
                mn = jnp.maximum(m, jnp.max(s, axis=1, keepdims=True))
                alpha = jnp.exp(m - mn)
                p = jnp.exp(s - mn)
                l = l * alpha + jnp.sum(p, axis=1, keepdims=True)
                acc = acc * alpha + jnp.dot(p.astype(BF16), vc, preferred_element_type=F32)
                return mn, l, acc

            def far(c, carry):
                c0 = pl.multiple_of(c * KCH, KCH)
                kc = k_ref[pl.ds(c0, KCH), ln]
                vc = v_ref[pl.ds(c0, KCH), ln]
                mb = jnp.concatenate([mbsc[c * (KCH // LANES) + t] for t in range(KCH // LANES)], axis=1)
                return update(carry, _dot_nt(qm, kc) + mb, vc)

            def near(kb, carry):
                k0 = pl.multiple_of(kb * LANES, LANES)
                kc = k_ref[pl.ds(k0, LANES), ln]
                vc = v_ref[pl.ds(k0, LANES), ln]
                wsel = jnp.clip(kb - j + 2, 0, 2)
                return update(carry, _dot_nt(qm, kc) + mbsc[kb] + bias_ref[wsel, h], vc)

            init = (jnp.full((QBLK, 1), NEG, F32), jnp.zeros((QBLK, 1), F32), jnp.zeros((QBLK, LANES), F32))
            carry = lax.fori_loop(0, nfull, far, init)
            _, l, acc = lax.fori_loop(nfull * (KCH // LANES), nk, near, carry)
            o_pair = jnp.where(hm, acc / l, o_pair)
        outs.append(o_pair)
    o_ref[...] = jnp.concatenate(outs, axis=1).astype(o_ref.dtype)


def _dsa_prompt(iq_bf, iw, q_bf, ik2_bf, k_bf, v_bf, bias3, batch, seq):
    n = batch * seq
    nqb = seq // QBLK
    topk = min(TOPK_MAX, seq // 4)
    assert seq % KCH == 0
    qblk = lambda wd: pl.BlockSpec((QBLK, wd), lambda b, j: (b * nqb + j, 0))
    whole = lambda wd: pl.BlockSpec((seq, wd), lambda b, j: (b, 0), pipeline_mode=pl.Buffered(1))
    return pl.pallas_call(
        functools.partial(_dsa_prompt_kernel, topk=topk),
        grid=(batch, nqb),
        in_specs=[qblk(AT_WIDTH), qblk(IDX_HEADS), qblk(AT_WIDTH), whole(LANES), whole(AT_WIDTH), whole(AT_WIDTH),
                  pl.BlockSpec((3, AT_HEADS, QBLK, LANES), lambda b, j: (0, 0, 0, 0))],
        out_specs=qblk(AT_WIDTH),
        out_shape=jax.ShapeDtypeStruct((n, AT_WIDTH), BF16),
        scratch_shapes=[pltpu.VMEM((seq // LANES, QBLK, LANES), I32),
                        pltpu.VMEM((seq // LANES, QBLK, LANES), F32),
                        pltpu.VMEM((IDX_HEADS, QBLK, LANES), F32),
                        pltpu.VMEM((IDX_HEADS, QBLK, LANES), BF16),
                        pltpu.VMEM((QBLK, LANES), I32),
                        pltpu.VMEM((QBLK, LANES), I32)],
        compiler_params=_cp(2),
        name="dsa_prompt",
    )(iq_bf, iw, q_bf, ik2_bf, k_bf, v_bf, bias3)


def _dsa_sel_kernel(pt_ref, iq_ref, w_ref, ikn_ref, cik_hbm, mb_ref, ikbuf, sem, *, layer, past, topk):
    b = pl.program_id(0)
    n_pages = past // PAGE
    lpad = past + PAGE

    def page_copy(pg):
        return pltpu.make_async_copy(cik_hbm.at[layer, pt_ref[b, pg]],
                                     ikbuf.at[pl.ds(pl.multiple_of(pg * PAGE, PAGE), PAGE), :], sem)

    def issue(pg, c):
        page_copy(pg).start()
        return c

    lax.fori_loop(0, n_pages, issue, 0)
    ikbuf[pl.ds(past, PAGE), :] = jnp.zeros((PAGE, IDX_HEAD), F32)
    ikbuf[pl.ds(past, 1), :] = ikn_ref[...]

    def wait(pg, c):
        page_copy(pg).wait()
        return c

    lax.fori_loop(0, n_pages, wait, 0)

    iq = iq_ref[...]
    w = (w_ref[...] * (IDX_HEADS ** -0.5)).astype(BF16).astype(F32)
    pieces = []
    step = 2048
    for c0 in list(range(0, past, step)) + [past]:
        wd = min(step, past - c0) if c0 < past else PAGE
        s = _dot_nt(iq, ikbuf[pl.ds(c0, wd), :])
        rl = (jnp.maximum(s, 0.0) * (IDX_HEAD ** -0.5)).astype(BF16).astype(F32)
        sc = jnp.sum(rl * w, axis=0, keepdims=True) + 0.0
        pieces.append(sc)
    score = jnp.concatenate(pieces, axis=1)
    pos = lax.broadcasted_iota(I32, (1, lpad), 1)
    key = jnp.where(pos <= past, _sortable(score), jnp.int32(INT_MIN))
    ktop = jnp.float32(topk)

    def count(pred):
        return jnp.sum(pred.astype(F32), axis=1, keepdims=True)

    def bisect(_, lohi):
        lo, hi = lohi
        mid = lo + lax.shift_right_logical(hi - lo, 1)
        ok = count(key >= mid) >= ktop
        return jnp.where(ok, mid, lo), jnp.where(ok, hi, mid)

    thr, _ = lax.fori_loop(0, 32, bisect, (jnp.full((1, 1), INT_MIN, I32), jnp.full((1, 1), INT_MAX, I32)))
    need = ktop - count(key > thr)

    def bis_j(_, lohi):
        lo_j, hi_j = lohi
        mid = lax.shift_right_arithmetic(lo_j + hi_j, 1)
        ok = count((key == thr) & (pos <= mid)) >= need
        return jnp.where(ok, lo_j, mid), jnp.where(ok, mid, hi_j)

    _, jlim = lax.fori_loop(0, 16, bis_j, (jnp.full((1, 1), -1, I32), jnp.full((1, 1), lpad - 1, I32)))
    sel = (key > thr) | ((key == thr) & (pos <= jlim))
    mb_ref[...] = jnp.where(sel & (pos <= past), 0.0, NEG)


def _dsa_sel(page_table, iq_bf, iw, ik_new, cache_idx_k, layer, past):
    db = iq_bf.shape[0]
    lpad = past + PAGE
    topk = min(TOPK_MAX, (past + 1) // 4)
    gs = pltpu.PrefetchScalarGridSpec(
        num_scalar_prefetch=1,
        grid=(db,),
        in_specs=[pl.BlockSpec((None, IDX_HEADS, IDX_HEAD), lambda b, pt: (b, 0, 0)),
                  pl.BlockSpec((None, IDX_HEADS, 1), lambda b, pt: (b, 0, 0)),
                  pl.BlockSpec((None, 1, IDX_HEAD), lambda b, pt: (b, 0, 0)),
                  pl.BlockSpec(memory_space=pl.ANY)],
        out_specs=pl.BlockSpec((None, 1, lpad), lambda b, pt: (b, 0, 0)),
        scratch_shapes=[pltpu.VMEM((lpad, IDX_HEAD), F32), pltpu.SemaphoreType.DMA(())],
    )
    return pl.pallas_call(
        functools.partial(_dsa_sel_kernel, layer=layer, past=past, topk=topk),
        grid_spec=gs,
        out_shape=jax.ShapeDtypeStruct((db, 1, lpad), F32),
        compiler_params=_cp(1),
        name="dsa_sample_select",
    )(page_table, iq_bf.reshape(db, IDX_HEADS, IDX_HEAD), iw.reshape(db, IDX_HEADS, 1),
      ik_new.reshape(db, 1, IDX_HEAD), cache_idx_k)


def _dsa_att_kernel(pt_ref, q_ref, k_ref, v_ref, mb_ref, bl_ref, kn_ref, vn_ref, mbn_ref, bn_ref,
                    segm_ref, o_ref, m_s, l_s, acc_s):
    p = pl.program_id(1)
    last = p == pl.num_programs(1) - 1
    segm = segm_ref[...]

    @pl.when(p == 0)
    def _():
        m_s[...] = jnp.full(m_s.shape, NEG, F32)
        l_s[...] = jnp.zeros(l_s.shape, F32)
        acc_s[...] = jnp.zeros(acc_s.shape, F32)

    q = q_ref[...] * (AT_HEAD ** -0.5)

    def accumulate(kp, vp, bias, mb):
        lg = _dot_x(kp * q, segm) + bias + mb
        m = m_s[...]
        mn = jnp.maximum(m, jnp.max(lg, axis=0, keepdims=True))
        alpha = jnp.exp(m - mn)
        pr = jnp.exp(lg - mn)
        l_s[...] = l_s[...] * alpha + jnp.sum(pr, axis=0, keepdims=True)
        acc_s[...] = acc_s[...] * alpha + jnp.sum(pr * vp, axis=0, keepdims=True)
        m_s[...] = mn

    accumulate(k_ref[...], v_ref[...], jnp.where(last, bl_ref[...], 0.0), mb_ref[...])

    @pl.when(last)
    def _():
        accumulate(kn_ref[...], vn_ref[...], bn_ref[...], mbn_ref[...])
        o_ref[...] = (acc_s[...] / l_s[...]).astype(o_ref.dtype)


def _dsa_att(page_table, q, k_new, v_new, mb, bias_last, bias_new, cache_k, cache_v, layer, past, segm):
    db = q.shape[0]
    n_pages = past // PAGE
    n_pool = cache_k.shape[1]
    ck = cache_k.reshape(cache_k.shape[0], n_pool, PAGE, AT_WIDTH)
    cv = cache_v.reshape(cache_v.shape[0], n_pool, PAGE, AT_WIDTH)
    mb_col = mb.reshape(db, past + PAGE, 1)
    row = lambda wd: pl.BlockSpec((None, 1, wd), lambda b, p, pt: (b, 0, 0))
    page = pl.BlockSpec((None, None, PAGE, AT_WIDTH), lambda b, p, pt: (layer, pt[b, p], 0, 0))
    gs = pltpu.PrefetchScalarGridSpec(
        num_scalar_prefetch=1,
        grid=(db, n_pages),
        in_specs=[row(AT_WIDTH), page, page,
                  pl.BlockSpec((None, PAGE, 1), lambda b, p, pt: (b, p, 0)),
                  pl.BlockSpec((PAGE, AT_WIDTH), lambda b, p, pt: (0, 0)),
                  row(AT_WIDTH), row(AT_WIDTH),
                  pl.BlockSpec((None, 1, 1), lambda b, p, pt: (b, 0, 0)),
                  pl.BlockSpec((1, AT_WIDTH), lambda b, p, pt: (0, 0)),
                  pl.BlockSpec((AT_WIDTH, AT_WIDTH), lambda b, p, pt: (0, 0))],
        out_specs=row(AT_WIDTH),
        scratch_shapes=[pltpu.VMEM((1, AT_WIDTH), F32), pltpu.VMEM((1, AT_WIDTH), F32),
                        pltpu.VMEM((1, AT_WIDTH), F32)],
    )
    out = pl.pallas_call(
        _dsa_att_kernel,
        grid_spec=gs,
        out_shape=jax.ShapeDtypeStruct((db, 1, AT_WIDTH), BF16),
        compiler_params=_cp(2),
        name="dsa_sample_attend",
    )(page_table, q.reshape(db, 1, AT_WIDTH), ck, cv, mb_col, bias_last,
      k_new.reshape(db, 1, AT_WIDTH), v_new.reshape(db, 1, AT_WIDTH),
      mb_col[:, past:past + 1, :], bias_new, segm)
    return out.reshape(db, AT_WIDTH)


def _mem_kernel(x_ref, w_ref, mk_ref, mv_ref, o_ref):
    mq = jnp.dot(x_ref[...].astype(BF16), w_ref[...], preferred_element_type=F32)
    outs = []
    for h in range(MEM_HEADS):
        ln = slice(h * MEM_HEAD, (h + 1) * MEM_HEAD)
        s = _dot_nt(mq[:, ln], mk_ref[:, ln]) * (MEM_HEAD ** -0.5)
        s = s - jnp.max(s, axis=1, keepdims=True)
        p = jnp.exp(s)
        p = p / jnp.sum(p, axis=1, keepdims=True)
        outs.append(_dot(p, mv_ref[:, ln]))
    o_ref[...] = jnp.concatenate(outs, axis=1).astype(o_ref.dtype)


def _mem_attend(x3, w_mq_bf, mk, mv, tm):
    bm, t, _ = x3.shape
    tm = min(tm, t)
    mlen = mk.shape[1]
    return pl.pallas_call(
        _mem_kernel,
        grid=(bm, t // tm),
        in_specs=[pl.BlockSpec((None, tm, D_MODEL), lambda b, i: (b, i, 0)),
                  pl.BlockSpec((D_MODEL, MEM_WIDTH), lambda b, i: (0, 0)),
                  pl.BlockSpec((None, mlen, MEM_WIDTH), lambda b, i: (b, 0, 0)),
                  pl.BlockSpec((None, mlen, MEM_WIDTH), lambda b, i: (b, 0, 0))],
        out_specs=pl.BlockSpec((None, tm, MEM_WIDTH), lambda b, i: (b, i, 0)),
        out_shape=jax.ShapeDtypeStruct((bm, t, MEM_WIDTH), BF16),
        compiler_params=_cp(2),
        name="mem_attend",
    )(x3, w_mq_bf, mk, mv)


def _merge_kernel(x_ref, yrw_ref, yat_ref, ymem_ref, wg_ref, wrw_ref, wat_ref, wmem_ref, wout_ref,
                  g1_ref, b1_ref, rw_ref, rb_ref, x1_ref, ti_ref, tg_ref, *, alpha):
    x = x_ref[...]
    gates = jax.nn.sigmoid(jnp.dot(x.astype(BF16), wg_ref[...], preferred_element_type=F32))
    d = D_MODEL
    merged = (gates[:, 0:d] * jnp.dot(yrw_ref[...], wrw_ref[...], preferred_element_type=F32)
              + gates[:, d:2 * d] * jnp.dot(yat_ref[...], wat_ref[...], preferred_element_type=F32)
              + gates[:, 2 * d:3 * d] * jnp.dot(ymem_ref[...], wmem_ref[...], preferred_element_type=F32))
    mix = jnp.dot(merged.astype(BF16), wout_ref[...], preferred_element_type=F32)
    x1 = _layer_norm(alpha * x + mix, g1_ref[...], b1_ref[...])
    x1_ref[...] = x1
    logits = _dot(x1, rw_ref[...]) + rb_ref[...]
    lane = lax.broadcasted_iota(I32, logits.shape, 1)
    idx_out = jnp.zeros(logits.shape, I32)
    val_out = jnp.full(logits.shape, NEG, F32)
    for r in range(TOP_K):
        m = jnp.max(logits, axis=1, keepdims=True)
        am = jnp.min(jnp.where(logits == m, lane, LANES), axis=1, keepdims=True)
        idx_out = jnp.where(lane == r, am, idx_out)
        val_out = jnp.where(lane == r, m, val_out)
        logits = jnp.where(lane == am, -jnp.inf, logits)
    e = jnp.exp(val_out - jnp.max(val_out, axis=1, keepdims=True))
    tg_ref[...] = e / jnp.sum(e, axis=1, keepdims=True)
    ti_ref[...] = idx_out


def _merge(x, y_rw, y_at, y_mem, lw, tm, alpha):
    n = x.shape[0]
    tm = min(tm, n)
    full = lambda a: pl.BlockSpec(a.shape, lambda i: (0,) * a.ndim)
    rowb = lambda wd: pl.BlockSpec((tm, wd), lambda i: (i, 0))
    ws = [lw["w_gate"], lw["w_br_rw"], lw["w_br_at"], lw["w_br_mem"], lw["w_out"],
          lw["ln1_g"], lw["ln1_b"], lw["router_w"], lw["router_b"]]
    return pl.pallas_call(
        functools.partial(_merge_kernel, alpha=alpha),
        grid=(n // tm,),
        in_specs=[rowb(D_MODEL), rowb(RW_WIDTH), rowb(AT_WIDTH), rowb(MEM_WIDTH)] + [full(a) for a in ws],
        out_specs=[rowb(D_MODEL), rowb(LANES), rowb(LANES)],
        out_shape=[jax.ShapeDtypeStruct((n, D_MODEL), F32), jax.ShapeDtypeStruct((n, LANES), I32),
                   jax.ShapeDtypeStruct((n, LANES), F32)],
        compiler_params=_cp(1),
        name="merge_ln_router",
    )(x, y_rw, y_at, y_mem, *ws)


def _ffn(xb, wg, bg, wu, bu, wd, bd):
    hg = jnp.minimum(jnp.dot(xb, wg.astype(BF16), preferred_element_type=F32) + bg, SWIGLU_LIMIT)
    hl = jnp.clip(jnp.dot(xb, wu.astype(BF16), preferred_element_type=F32) + bu, -SWIGLU_LIMIT, SWIGLU_LIMIT)
    h = hg * jax.nn.sigmoid(SWIGLU_ALPHA * hg) * (hl + 1.0)
    return jnp.dot(h.astype(BF16), wd.astype(BF16), preferred_element_type=F32) + bd


def _moe_expert_kernel(be_ref, nu_ref, tok_ref, x_hbm, gate_ref, wg_ref, bg_ref, wu_ref, bu_ref, wd_ref, bd_ref,
                       o_ref, xbuf, sem, *, blk):
    i = pl.program_id(0)

    def row_copy(r, tok):
        return pltpu.make_async_copy(x_hbm.at[pl.ds(tok, 1), :], xbuf.at[pl.ds(r, 1), :], sem)

    @pl.when(i < nu_ref[0])
    def _():
        def issue(r, c):
            row_copy(r, tok_ref[0, 0, r]).start()
            return c

        lax.fori_loop(0, blk, issue, 0, unroll=8)

        def wait(r, c):
            row_copy(r, 0).wait()
            return c

        lax.fori_loop(0, blk, wait, 0, unroll=8)
        out = _ffn(xbuf[...].astype(BF16), wg_ref[...], bg_ref[...], wu_ref[...], bu_ref[...],
                   wd_ref[...], bd_ref[...])
        o_ref[...] = out * gate_ref[...]

    @pl.when(i >= nu_ref[0])
    def _():
        o_ref[...] = jnp.zeros(o_ref.shape, F32)


def _moe_experts(blk_exp, n_used, buf_tok, x1, buf_gate, moe, layer, blk):
    n_blocks = blk_exp.shape[0]
    cap = n_blocks * blk
    wspec = pl.BlockSpec((None, None, D_MODEL, D_FF), lambda i, be, nu: (layer, be[i], 0, 0))
    wdspec = pl.BlockSpec((None, None, D_FF, D_MODEL), lambda i, be, nu: (layer, be[i], 0, 0))
    bspec = pl.BlockSpec((None, None, 1, D_FF), lambda i, be, nu: (layer, be[i], 0, 0))
    gs = pltpu.PrefetchScalarGridSpec(
        num_scalar_prefetch=2,
        grid=(n_blocks,),
        in_specs=[pl.BlockSpec((1, 1, blk), lambda i, be, nu: (i, 0, 0), memory_space=pltpu.SMEM),
                  pl.BlockSpec(memory_space=pl.ANY),
                  pl.BlockSpec((blk, 1), lambda i, be, nu: (i, 0)),
                  wspec, bspec, wspec, bspec, wdspec, bspec],
        out_specs=pl.BlockSpec((blk, D_MODEL), lambda i, be, nu: (i, 0)),
        scratch_shapes=[pltpu.VMEM((blk, D_MODEL), F32), pltpu.SemaphoreType.DMA(())],
    )
    e = N_EXPERTS
    depth = moe["w_gate"].shape[0]
    b4 = lambda a: a.reshape(depth, e, 1, a.shape[-1])
    return pl.pallas_call(
        functools.partial(_moe_expert_kernel, blk=blk),
        grid_spec=gs,
        out_shape=jax.ShapeDtypeStruct((cap, D_MODEL), F32),
        compiler_params=_cp(1),
        name="moe_experts",
    )(blk_exp, n_used, buf_tok.reshape(n_blocks, 1, blk), x1, buf_gate.reshape(cap, 1),
      moe["w_gate"], b4(moe["b_gate"]), moe["w_up"], b4(moe["b_up"]), moe["w_down"], b4(moe["b_down"]))


def _moe_combine_kernel(dest_ref, o_hbm, x1_ref, g2_ref, b2_ref, y_ref, buf, sem, *, tmc, alpha):
    def row_copy(jj, t, d):
        return pltpu.make_async_copy(o_hbm.at[pl.ds(d, 1), :], buf.at[jj, pl.ds(t, 1), :], sem)

    for jj in range(TOP_K):
        def issue(t, c):
            row_copy(jj, t, dest_ref[0, jj, t]).start()
            return c

        lax.fori_loop(0, tmc, issue, 0, unroll=8)
    for jj in range(TOP_K):
        def wait(t, c):
            row_copy(jj, t, 0).wait()
            return c

        lax.fori_loop(0, tmc, wait, 0, unroll=8)
    y = (buf[0] + buf[1]) + (buf[2] + buf[3])
    y_ref[...] = _layer_norm(alpha * x1_ref[...] + y, g2_ref[...], b2_ref[...])


def _moe_combine(dest, out_sorted, x1, ln2_g, ln2_b, tmc, alpha):
    n = x1.shape[0]
    nt = n // tmc
    dest3 = dest.reshape(nt, tmc, TOP_K).transpose(0, 2, 1)
    return pl.pallas_call(
        functools.partial(_moe_combine_kernel, tmc=tmc, alpha=alpha),
        grid=(nt,),
        in_specs=[pl.BlockSpec((1, TOP_K, tmc), lambda i: (i, 0, 0), memory_space=pltpu.SMEM),
                  pl.BlockSpec(memory_space=pl.ANY),
                  pl.BlockSpec((tmc, D_MODEL), lambda i: (i, 0)),
                  pl.BlockSpec((1, D_MODEL), lambda i: (0, 0)),
                  pl.BlockSpec((1, D_MODEL), lambda i: (0, 0))],
        out_specs=pl.BlockSpec((tmc, D_MODEL), lambda i: (i, 0)),
        out_shape=jax.ShapeDtypeStruct((n, D_MODEL), F32),
        scratch_shapes=[pltpu.VMEM((TOP_K, tmc, D_MODEL), F32), pltpu.SemaphoreType.DMA(())],
        compiler_params=_cp(1),
        name="moe_combine_ln",
    )(dest3, out_sorted, x1, ln2_g, ln2_b)


def _moe_prompt(x1, top_idx, top_gate, moe, layer, ln2_g, ln2_b, alpha, blk, tmc):
    n = x1.shape[0]
    n_asg = n * TOP_K
    e = N_EXPERTS
    n_blocks = n_asg // blk + e
    cap = n_blocks * blk
    expert = top_idx.reshape(n_asg)
    token = jnp.repeat(jnp.arange(n, dtype=I32), TOP_K)
    order = jnp.argsort(expert, stable=True)
    e_s = expert[order]
    counts = jnp.bincount(expert, length=e)
    starts = jnp.cumsum(counts) - counts
    padded = (counts + blk - 1) // blk * blk
    pends = jnp.cumsum(padded)
    pstarts = pends - padded
    dest_sorted = (pstarts[e_s] + jnp.arange(n_asg) - starts[e_s]).astype(I32)
    buf_tok = jnp.zeros((cap,), I32).at[dest_sorted].set(token[order])
    buf_gate = jnp.zeros((cap,), F32).at[dest_sorted].set(top_gate.reshape(n_asg)[order])
    blk_exp = jnp.minimum(jnp.searchsorted(pends, jnp.arange(n_blocks) * blk, side="right"), e - 1).astype(I32)
    n_used = (pends[-1] // blk).astype(I32).reshape(1)
    dest = jnp.zeros((n_asg,), I32).at[order].set(dest_sorted)
    out_sorted = _moe_experts(blk_exp, n_used, buf_tok, x1, buf_gate, moe, layer, blk)
    return _moe_combine(dest, out_sorted, x1, ln2_g, ln2_b, tmc, alpha)


def _moe_dense_kernel(x_ref, ti_ref, tg_ref, wg_ref, bg_ref, wu_ref, bu_ref, wd_ref, bd_ref, g2_ref, b2_ref,
                      y_ref, acc, *, alpha):
    e = pl.program_id(0)

    @pl.when(e == 0)
    def _():
        acc[...] = jnp.zeros(acc.shape, F32)

    x = x_ref[...]
    ti = ti_ref[...]
    tg = tg_ref[...]
    lane = lax.broadcasted_iota(I32, ti.shape, 1)
    cw = jnp.sum(jnp.where((ti == e) & (lane < TOP_K), tg, 0.0), axis=1, keepdims=True)
    out = _ffn(x.astype(BF16), wg_ref[...], bg_ref[...], wu_ref[...], bu_ref[...], wd_ref[...], bd_ref[...])
    acc[...] = acc[...] + cw * out

    @pl.when(e == pl.num_programs(0) - 1)
    def _():
        y_ref[...] = _layer_norm(alpha * x + acc[...], g2_ref[...], b2_ref[...])


def _moe_dense(x1, top_idx, top_gate, moe, layer, ln2_g, ln2_b, alpha):
    n = x1.shape[0]
    e = N_EXPERTS
    depth = moe["w_gate"].shape[0]
    b4 = lambda a: a.reshape(depth, e, 1, a.shape[-1])
    full = lambda a: pl.BlockSpec(a.shape, lambda i: (0,) * a.ndim)
    wspec = pl.BlockSpec((None, None, D_MODEL, D_FF), lambda i: (layer, i, 0, 0))
    wdspec = pl.BlockSpec((None, None, D_FF, D_MODEL), lambda i: (layer, i, 0, 0))
    bspec = pl.BlockSpec((None, None, 1, D_FF), lambda i: (layer, i, 0, 0))
    return pl.pallas_call(
        functools.partial(_moe_dense_kernel, alpha=alpha),
        grid=(e,),
        in_specs=[full(x1), full(top_idx), full(top_gate), wspec, bspec, wspec, bspec, wdspec, bspec,
                  full(ln2_g), full(ln2_b)],
        out_specs=pl.BlockSpec((n, D_MODEL), lambda i: (0, 0)),
        out_shape=jax.ShapeDtypeStruct((n, D_MODEL), F32),
        scratch_shapes=[pltpu.VMEM((n, D_MODEL), F32)],
        compiler_params=_cp(1),
        name="moe_dense",
    )(x1, top_idx, top_gate, moe["w_gate"], b4(moe["b_gate"]), moe["w_up"], b4(moe["b_up"]),
      moe["w_down"], b4(moe["b_down"]), ln2_g, ln2_b)


def _t5_bucket(rel):
    n = jnp.maximum(rel, 0)
    max_exact = N_BUCKETS // 2
    nf = jnp.maximum(n, 1).astype(F32)
    large = max_exact + (jnp.log(nf / max_exact) / math.log(MAX_DISTANCE / max_exact)
                         * (N_BUCKETS - max_exact)).astype(I32)
    large = jnp.minimum(large, N_BUCKETS - 1)
    return jnp.where(n < max_exact, n, large)


def _const_mats():
    hid = np.arange(RW_WIDTH) // RW_HEAD
    segm = (hid[:, None] == hid[None, :]).astype(np.float32)
    ltri = np.tril(np.ones((CHUNK, CHUNK), np.float32))
    return jnp.asarray(segm, BF16), jnp.asarray(ltri, BF16)


def _layer_weights(l, w_in, rw_mu, rw_w0, rw_wB, rw_a0, rw_aB, rw_gB, rw_kk, rw_ka, rw_rk, rw_gn_g, rw_gn_b,
                   mem_wk, mem_wv, w_br_rw, w_br_at, w_br_mem, w_out, ln1_g, ln1_b, ln2_g, ln2_b,
                   router_w, router_b, segm, ltri):
    wl = w_in[l]
    o_at = RW_IN
    o_mq = RW_IN + AT_IN
    o_g = o_mq + MEM_WIDTH
    w_at = wl[:, o_at:o_at + AT_IN]
    ikc = w_at[:, 4 * AT_WIDTH:4 * AT_WIDTH + IDX_HEAD]
    iwc = w_at[:, 4 * AT_WIDTH + IDX_HEAD:]
    w_at2 = jnp.concatenate([w_at[:, :4 * AT_WIDTH], ikc, ikc, iwc,
                             jnp.zeros((D_MODEL, LANES - IDX_HEADS), F32)], axis=1)
    row = lambda a: a[l].reshape(1, -1)
    zpad = jnp.zeros((RW_LORA_W, RW_WIDTH), F32)
    return {
        "w_rw": wl[:, :RW_IN].astype(BF16),
        "w_at": w_at2.astype(BF16),
        "w_mq": wl[:, o_mq:o_g].astype(BF16),
        "w_gate": wl[:, o_g:].astype(BF16),
        "w_mem": jnp.concatenate([mem_wk[l], mem_wv[l]], axis=1).astype(BF16),
        "mu": row(rw_mu), "w0": row(rw_w0), "a0": row(rw_a0), "kk": row(rw_kk), "ka": row(rw_ka),
        "rk": row(rw_rk), "gn_g": row(rw_gn_g), "gn_b": row(rw_gn_b),
        "wB2": jnp.concatenate([rw_wB[l], zpad], axis=0).astype(BF16),
        "aB2": jnp.concatenate([zpad, rw_aB[l]], axis=0).astype(BF16),
        "gB": rw_gB[l].astype(BF16),
        "segm": segm, "ltri": ltri,
        "w_br_rw": w_br_rw[l].astype(BF16), "w_br_at": w_br_at[l].astype(BF16),
        "w_br_mem": w_br_mem[l].astype(BF16), "w_out": w_out[l].astype(BF16),
        "ln1_g": row(ln1_g), "ln1_b": row(ln1_b), "ln2_g": row(ln2_g), "ln2_b": row(ln2_b),
        "router_w": jnp.concatenate([router_w[l], jnp.zeros((D_MODEL, LANES - N_EXPERTS), F32)], axis=1),
        "router_b": jnp.concatenate([router_b[l], jnp.full((LANES - N_EXPERTS,), NEG, F32)]).reshape(1, LANES),
    }


_AT_COLS = [(0, 512, AT_HEAD ** -0.5), (512, 512, 1.0), (1024, 512, 1.0), (512, 512, 1.0), (1024, 512, 1.0),
            (1536, 512, 1.0), (2048, 64, 1.0), (2048, 128, 1.0), (2176, 8, 1.0)]
_AT_DTYPES = [BF16, F32, F32, BF16, BF16, BF16, F32, BF16, F32]


def kernel(x_prompt, x_sample, mem_prompt, cache_k, cache_v, cache_idx_k, page_table, cache_mem_k, cache_mem_v, state_rwkv, state_shift, w_in, rw_mu, rw_w0, rw_wB, rw_a0, rw_aB, rw_gB, rw_kk, rw_ka, rw_rk, rw_gn_g, rw_gn_b, mem_wk, mem_wv, rel_bias, w_br_rw, w_br_at, w_br_mem, w_out, ln1_g, ln1_b, ln2_g, ln2_b, router_w, router_b, moe_w_gate, moe_b_gate, moe_w_up, moe_b_up, moe_w_down, moe_b_down):
    depth = w_in.shape[0]
    bp, seq, _ = x_prompt.shape
    db = x_sample.shape[0]
    n = bp * seq
    mlen = mem_prompt.shape[1]
    past = page_table.shape[1] * PAGE
    alpha = (2 * depth) ** 0.25
    segm, ltri = _const_mats()
    moe = {"w_gate": moe_w_gate, "b_gate": moe_b_gate, "w_up": moe_w_up, "b_up": moe_b_up,
           "w_down": moe_w_down, "b_down": moe_b_down}

    bias_d = rel_bias[_t5_bucket(jnp.arange(2 * QBLK))]
    far = rel_bias[N_BUCKETS - 1]
    ri = np.arange(QBLK)[:, None]
    ci = np.arange(LANES)[None, :]
    prev_tile = (bias_d[QBLK + ri - ci] - far).transpose(2, 0, 1)
    diag_tile = (bias_d[np.maximum(ri - ci, 0)] - far).transpose(2, 0, 1)
    bias3 = jnp.stack([jnp.zeros_like(prev_tile), prev_tile, diag_tile])
    near_s = rel_bias[_t5_bucket(past - jnp.arange(past - PAGE, past + 1))] - far
    near_s = jnp.repeat(near_s, AT_HEAD, axis=1)
    bias_last, bias_new = near_s[:PAGE], near_s[PAGE:]

    xp = x_prompt.reshape(n, D_MODEL)
    xs = x_sample.reshape(db, D_MODEL)
    tm_rw = min(256, seq)
    acc = {k: [] for k in ("kp", "vp", "ikp", "mkp", "mvp", "sp", "shp", "ks", "vs", "iks", "ss", "shs")}

    for l in range(depth):
        lw = _layer_weights(l, w_in, rw_mu, rw_w0, rw_wB, rw_a0, rw_aB, rw_gB, rw_kk, rw_ka, rw_rk, rw_gn_g,
                            rw_gn_b, mem_wk, mem_wv, w_br_rw, w_br_at, w_br_mem, w_out, ln1_g, ln1_b,
                            ln2_g, ln2_b, router_w, router_b, segm, ltri)

        acc["shp"].append(xp.reshape(bp, seq, D_MODEL)[:, -1])
        mk, mv = _proj(mem_prompt.reshape(bp * mlen, D_MODEL), lw["w_mem"],
                       [(0, MEM_WIDTH, 1.0), (MEM_WIDTH, MEM_WIDTH, 1.0)], [F32, F32], 256)
        acc["mkp"].append(mk.reshape(bp, mlen, MEM_HEADS, MEM_HEAD))
        acc["mvp"].append(mv.reshape(bp, mlen, MEM_HEADS, MEM_HEAD))
        (p_rw,) = _proj(xp, lw["w_rw"], [(0, RW_IN, 1.0)], [F32], 512)
        q_bf, k32, v32, k_bf, v_bf, iq_bf, ik32, ik2_bf, iw = _proj(xp, lw["w_at"], _AT_COLS, _AT_DTYPES, 512)
        acc["kp"].append(k32.reshape(bp, seq, AT_HEADS, AT_HEAD))
        acc["vp"].append(v32.reshape(bp, seq, AT_HEADS, AT_HEAD))
        acc["ikp"].append(ik32.reshape(bp, seq, IDX_HEAD))

        y1, y2, bonus, g, ac, cc = _rw_chunks(p_rw, jnp.zeros((bp, 1, RW_IN), F32), lw, bp, seq, tm_rw)
        y_rw, hfin = _rw_scan(y1, y2, bonus, g, ac, cc, jnp.zeros((bp, LANES, RW_WIDTH), F32), lw, bp, seq, tm_rw)
        hf = hfin.reshape(bp, 2, RW_HEAD, RW_HEADS // 2, 2, RW_HEAD)
        s_fin = jnp.stack([hf[:, hh, :, :, hh, :] for hh in range(2)], axis=3)
        acc["sp"].append(s_fin.reshape(bp, RW_HEAD, RW_HEADS, RW_HEAD).transpose(0, 2, 3, 1))

        y_at = _dsa_prompt(iq_bf, iw, q_bf, ik2_bf, k_bf, v_bf, bias3, bp, seq)
        y_mem = _mem_attend(xp.reshape(bp, seq, D_MODEL), lw["w_mq"], mk.reshape(bp, mlen, MEM_WIDTH),
                            mv.reshape(bp, mlen, MEM_WIDTH), 512).reshape(n, MEM_WIDTH)
        x1, ti, tg = _merge(xp, y_rw, y_at, y_mem, lw, 256, alpha)
        xp = _moe_prompt(x1, ti[:, :TOP_K], tg[:, :TOP_K], moe, l, lw["ln2_g"], lw["ln2_b"], alpha, 256, 128)

        acc["shs"].append(xs)
        (p2,) = _proj(jnp.concatenate([xs, state_shift[l]], axis=0), lw["w_rw"], [(0, RW_IN, 1.0)], [F32], 2 * db)
        qs, ks32, vs32, _, _, iqs_bf, iks32, _, iws = _proj(xs, lw["w_at"], [(0, 512, 1.0)] + _AT_COLS[1:],
                                                            [F32] + _AT_DTYPES[1:], db)
        acc["ks"].append(ks32.reshape(db, 1, AT_HEADS, AT_HEAD))
        acc["vs"].append(vs32.reshape(db, 1, AT_HEADS, AT_HEAD))
        acc["iks"].append(iks32.reshape(db, 1, IDX_HEAD))
        r_, k_, v_, kk_, b_, w_, g_ = _rw_step_prep(p2[:db], p2[db:], lw)
        s_new, y_rw_s = _rw_step(state_rwkv[l], r_, w_, k_, kk_, b_, v_, g_, lw)
        acc["ss"].append(s_new)
        mb = _dsa_sel(page_table, iqs_bf, iws, iks32, cache_idx_k, l, past)
        y_at_s = _dsa_att(page_table, qs, ks32, vs32, mb, bias_last, bias_new, cache_k, cache_v, l, past, segm)
        y_mem_s = _mem_attend(xs.reshape(db, 1, D_MODEL), lw["w_mq"], cache_mem_k[l].reshape(db, mlen, MEM_WIDTH),
                              cache_mem_v[l].reshape(db, mlen, MEM_WIDTH), 1).reshape(db, MEM_WIDTH)
        x1s, tis, tgs = _merge(xs, y_rw_s.astype(BF16), y_at_s, y_mem_s, lw, db, alpha)
        xs = _moe_dense(x1s, tis, tgs, moe, l, lw["ln2_g"], lw["ln2_b"], alpha)

    st = lambda k: jnp.stack(acc[k])
    return (xp.reshape(bp, seq, D_MODEL), xs.reshape(db, 1, D_MODEL),
            st("kp"), st("vp"), st("ikp"), st("mkp"), st("mvp"), st("sp"), st("shp"),
            st("ks"), st("vs"), st("iks"), st("ss"), st("shs"))
```

```python
import functools
import math

import numpy as np
import jax
import jax.numpy as jnp
from jax import lax
from jax.experimental import pallas as pl
from jax.experimental.pallas import tpu as pltpu

F32 = jnp.float32
BF16 = jnp.bfloat16
I32 = jnp.int32

D_MODEL = 1024
RW_HEADS, RW_HEAD, RW_WIDTH = 8, 64, 512
RW_LORA_W, RW_LORA_A, RW_LORA_G = 64, 64, 128
RW_IN = 3 * RW_WIDTH + RW_LORA_W + RW_LORA_A + RW_LORA_G
RW_GN_EPS = 64e-5
AT_HEADS, AT_HEAD, AT_WIDTH = 8, 64, 512
IDX_HEADS, IDX_HEAD = 8, 64
AT_IN = 3 * AT_WIDTH + IDX_HEADS * IDX_HEAD + IDX_HEAD + IDX_HEADS
TOPK_MAX = 256
N_BUCKETS, MAX_DISTANCE = 32, 128
MEM_HEADS, MEM_HEAD, MEM_WIDTH = 4, 128, 512
N_EXPERTS, TOP_K, D_FF = 32, 4, 1024
SWIGLU_LIMIT, SWIGLU_ALPHA = 7.0, 1.702
LN_EPS = 1e-5
PAGE = 128

LANES = 128
CHUNK = 64
QBLK = 128
KCH = 512
NEG = -1e30
INT_MIN = -(2 ** 31)
INT_MAX = 2 ** 31 - 1
VMEM_LIMIT = 56 * 1024 * 1024


def _cp(n_axes, vmem=VMEM_LIMIT):
    return pltpu.CompilerParams(dimension_semantics=("arbitrary",) * n_axes,
                                vmem_limit_bytes=vmem)


def _dot(a, b):
    return jnp.dot(a.astype(BF16), b.astype(BF16), preferred_element_type=F32)


def _dot_nt(a, b):
    return lax.dot_general(a.astype(BF16), b.astype(BF16), (((1,), (1,)), ((), ())),
                           preferred_element_type=F32)


def _dot_tn(a, b):
    return lax.dot_general(a.astype(BF16), b.astype(BF16), (((0,), (0,)), ((), ())),
                           preferred_element_type=F32)


def _split2(a):
    hi = a.astype(BF16)
    lo = (a - hi.astype(F32)).astype(BF16)
    return hi, lo


def _split3(a):
    hi = a.astype(BF16)
    r1 = a - hi.astype(F32)
    mid = r1.astype(BF16)
    lo = (r1 - mid.astype(F32)).astype(BF16)
    return hi, mid, lo


def _dot_x(a, b_exact):
    hi, mid, lo = _split3(a)
    d = lambda x: jnp.dot(x, b_exact, preferred_element_type=F32)
    return d(hi) + d(mid) + d(lo)


def _xdot(b_exact, a):
    hi, mid, lo = _split3(a)
    d = lambda x: jnp.dot(b_exact, x, preferred_element_type=F32)
    return d(hi) + d(mid) + d(lo)


def _dot_hp(a, b):
    ah, al = _split2(a)
    bh, bl = _split2(b)
    d = lambda x, y: jnp.dot(x, y, preferred_element_type=F32)
    return d(ah, bh) + d(ah, bl) + d(al, bh)


def _layer_norm(z, g, b):
    mu = jnp.mean(z, axis=-1, keepdims=True)
    zc = z - mu
    var = jnp.mean(zc * zc, axis=-1, keepdims=True)
    return zc * lax.rsqrt(var + LN_EPS) * g + b


def _head_mask(hh):
    lane = lax.broadcasted_iota(I32, (1, LANES), 1)
    return (lane >= hh * 64) & (lane < (hh + 1) * 64)


def _proj_kernel(x_ref, w_ref, *o_refs, cols):
    acc = jnp.dot(x_ref[...].astype(BF16), w_ref[...], preferred_element_type=F32)
    for o_ref, (c0, wd, scale) in zip(o_refs, cols):
        piece = acc[:, c0:c0 + wd]
        if scale != 1.0:
            piece = piece * scale
        o_ref[...] = piece.astype(o_ref.dtype)


def _proj(x, w_bf, cols, dtypes, tm):
    n, kdim = x.shape
    m = w_bf.shape[1]
    tm = min(tm, n)
    grid = (n // tm,)
    return pl.pallas_call(
        functools.partial(_proj_kernel, cols=tuple(cols)),
        grid=grid,
        in_specs=[pl.BlockSpec((tm, kdim), lambda i: (i, 0)),
                  pl.BlockSpec((kdim, m), lambda i: (0, 0))],
        out_specs=[pl.BlockSpec((tm, wd), lambda i: (i, 0)) for (_, wd, _) in cols],
        out_shape=[jax.ShapeDtypeStruct((n, wd), dt) for (_, wd, _), dt in zip(cols, dtypes)],
        compiler_params=_cp(1),
        name="proj",
    )(x, w_bf)


def _rw_prep(p, p_prev, mu, w0, wB2, a0, aB2, gB, kkp, ka, segm):
    xs = p + (p_prev - p) * mu
    r = xs[:, 0:512]
    k = xs[:, 512:1024]
    v = xs[:, 1024:1536]
    lwla = xs[:, 1536:1664]
    lg = xs[:, 1664:1792]
    w = -jax.nn.softplus(-(w0 + _dot(jnp.tanh(lwla), wB2))) - 0.5
    ld = -jnp.exp(w)
    a = jax.nn.sigmoid(a0 + _dot(lwla, aB2))
    g = _dot(jax.nn.sigmoid(lg), gB)
    kkr = k * kkp
    ss = _dot_x(kkr * kkr, segm)
    kk = kkr / jnp.maximum(jnp.sqrt(ss), 1e-12)
    k2 = k * (1.0 + (a - 1.0) * ka)
    b = kk * a
    return r, k2, v, kk, b, ld, g


def _rw_chunk_kernel(p_ref, pb_ref, p0_ref, mu_ref, w0_ref, wB_ref, a0_ref, aB_ref, gB_ref,
                     kk_ref, ka_ref, rk_ref, segm_ref, ltri_ref,
                     y1_ref, y2_ref, bonus_ref, g_ref, ac_ref, cc_ref,
                     r_s, k_s, v_s, kk_s, b_s, ld_s, *, tm):
    i = pl.program_id(1)
    p = p_ref[...]
    prev_last = jnp.where(i == 0, p0_ref[...], pb_ref[7:8, :])
    rolled = pltpu.roll(p, 1, axis=0)
    row = lax.broadcasted_iota(I32, p.shape, 0)
    p_prev = jnp.where(row == 0, prev_last, rolled)
    segm = segm_ref[...]
    r, k2, v, kk, b, ld, g = _rw_prep(p, p_prev, mu_ref[...], w0_ref[...], wB_ref[...], a0_ref[...],
                                      aB_ref[...], gB_ref[...], kk_ref[...], ka_ref[...], segm)
    g_ref[...] = g
    bonus_ref[...] = _dot_x(r * k2 * rk_ref[...], segm) * v
    r_s[...] = r
    k_s[...] = k2
    v_s[...] = v
    kk_s[...] = kk
    b_s[...] = b
    ld_s[...] = ld

    ltri = ltri_ref[...]
    ri = lax.broadcasted_iota(I32, (CHUNK, CHUNK), 0)
    ci = lax.broadcasted_iota(I32, (CHUNK, CHUNK), 1)
    strict = ri > ci
    incl = ri >= ci
    eye = (lax.broadcasted_iota(I32, (LANES, LANES), 0) == lax.broadcasted_iota(I32, (LANES, LANES), 1))

    def chunk(c, carry):
        rows = pl.ds(pl.multiple_of(c * CHUNK, CHUNK), CHUNK)
        ldc = ld_s[rows, :]
        cum = _xdot(ltri, ldc)
        cum_c = cum[CHUNK - 1:CHUNK, :]
        pin = jnp.exp(cum)
        pex = jnp.exp(cum - ldc)
        pinv = jnp.exp(-cum)
        pend = jnp.exp(cum_c - cum)
        pc = jnp.exp(cum_c)
        kkc, bc, kc, rc, vc = kk_s[rows, :], b_s[rows, :], k_s[rows, :], r_s[rows, :], v_s[rows, :]
        at = -kkc * pex
        rt = rc * pin
        bt = bc * pinv
        kt = kc * pinv
        bend = bc * pend
        kend = kc * pend
        for pr in range(RW_HEADS // 2):
            ln = slice(pr * LANES, (pr + 1) * LANES)
            atp, rtp, btp, ktp, vp = at[:, ln], rt[:, ln], bt[:, ln], kt[:, ln], vc[:, ln]
            ar = jnp.concatenate([atp, rtp], axis=0)
            bkend = jnp.concatenate([bend[:, ln], kend[:, ln]], axis=0)
            zv = jnp.concatenate([jnp.zeros_like(vp), vp], axis=1)
            acp = jnp.where(eye, pc[:, ln], 0.0)
            ccp = jnp.zeros((LANES, LANES), F32)
            y1p = jnp.zeros((CHUNK, LANES), F32)
            y2p = jnp.zeros((CHUNK, LANES), F32)
            for hh in range(2):
                hm = _head_mask(hh)
                arm = jnp.where(hm, ar, 0.0)
                gb = _dot_nt(arm, btp)
                gk = _dot_nt(arm, ktp)
                nm = jnp.where(strict, gb[:CHUNK], 0.0)
                lrb = jnp.where(incl, gb[CHUNK:], 0.0)
                aak = jnp.where(strict, gk[:CHUNK], 0.0)
                lrk = jnp.where(incl, gk[CHUNK:], 0.0)
                x = jnp.concatenate([atp, _dot(aak, vp)], axis=1)
                n2 = _dot(nm, nm)
                n4 = _dot(n2, n2)
                n8 = _dot(n4, n4)
                n16 = _dot(n8, n8)
                n32 = _dot(n16, n16)
                for q in (n32, n16, n8, n4, n2, nm):
                    x = x + _dot(q, x)
                lhs = jnp.where(hm, bkend, 0.0)
                rhs = jnp.concatenate([x, zv], axis=0)
                res = _dot_tn(lhs, rhs)
                acp = acp + jnp.where(hm, res[:, :LANES], 0.0)
                ccp = ccp + jnp.where(hm, res[:, LANES:], 0.0)
                lx = _dot(lrb, x)
                y1p = y1p + jnp.where(hm, rtp + lx[:, :LANES], 0.0)
                y2p = y2p + jnp.where(hm, lx[:, LANES:] + _dot(lrk, vp), 0.0)
            y1_ref[rows, ln] = y1p
            y2_ref[rows, ln] = y2p
            ac_ref[c, :, ln] = acp
            cc_ref[c, :, ln] = ccp
        return carry

    lax.fori_loop(0, tm // CHUNK, chunk, 0)


def _rw_chunks(p_rw, prev0, lw, batch, seq, tm):
    n = batch * seq
    nt = seq // tm
    ncs = tm // CHUNK
    full = lambda shape: pl.BlockSpec(shape, lambda b, i: (0,) * len(shape))
    row_blk = pl.BlockSpec((tm, RW_WIDTH), lambda b, i: (b * nt + i, 0))
    mat_blk = pl.BlockSpec((ncs, LANES, RW_WIDTH), lambda b, i: (b * nt + i, 0, 0))
    return pl.pallas_call(
        functools.partial(_rw_chunk_kernel, tm=tm),
        grid=(batch, nt),
        in_specs=[pl.BlockSpec((tm, RW_IN), lambda b, i: (b * nt + i, 0)),
                  pl.BlockSpec((8, RW_IN), lambda b, i: (jnp.maximum((b * nt + i) * (tm // 8) - 1, 0), 0)),
                  pl.BlockSpec((None, 1, RW_IN), lambda b, i: (b, 0, 0)),
                  full((1, RW_IN)), full((1, RW_WIDTH)), full((LANES, RW_WIDTH)), full((1, RW_WIDTH)),
                  full((LANES, RW_WIDTH)), full((LANES, RW_WIDTH)), full((1, RW_WIDTH)), full((1, RW_WIDTH)),
                  full((1, RW_WIDTH)), full((RW_WIDTH, RW_WIDTH)), full((CHUNK, CHUNK))],
        out_specs=[row_blk, row_blk, row_blk, row_blk, mat_blk, mat_blk],
        out_shape=[jax.ShapeDtypeStruct((n, RW_WIDTH), F32)] * 4
                  + [jax.ShapeDtypeStruct((n // CHUNK, LANES, RW_WIDTH), F32)] * 2,
        scratch_shapes=[pltpu.VMEM((tm, RW_WIDTH), F32)] * 6,
        compiler_params=_cp(2),
        name="rwkv_chunks",
    )(p_rw, p_rw, prev0, lw["mu"], lw["w0"], lw["wB2"], lw["a0"], lw["aB2"], lw["gB"],
      lw["kk"], lw["ka"], lw["rk"], lw["segm"], lw["ltri"])


def _rw_scan_kernel(y1_ref, y2_ref, bonus_ref, g_ref, ac_ref, cc_ref, h0_ref, gng_ref, gnb_ref, segm_ref,
                    y_ref, hfin_ref, h_s, *, ncs):
    i = pl.program_id(1)

    @pl.when(i == 0)
    def _():
        h_s[...] = h0_ref[...]

    segm = segm_ref[...]
    for c in range(ncs):
        rows = slice(c * CHUNK, (c + 1) * CHUNK)
        parts = []
        for pr in range(RW_HEADS // 2):
            ln = slice(pr * LANES, (pr + 1) * LANES)
            hp = h_s[:, ln]
            parts.append(_dot_hp(y1_ref[rows, ln], hp) + y2_ref[rows, ln])
            h_s[:, ln] = _dot_hp(ac_ref[c, :, ln], hp) + cc_ref[c, :, ln]
        y = jnp.concatenate(parts, axis=1)
        mu = _dot_x(y, segm) * (1.0 / RW_HEAD)
        yc = y - mu
        var = _dot_x(yc * yc, segm) * (1.0 / RW_HEAD)
        yn = yc * lax.rsqrt(var + RW_GN_EPS) * gng_ref[...] + gnb_ref[...]
        y_ref[rows, :] = ((yn + bonus_ref[rows, :]) * g_ref[rows, :]).astype(y_ref.dtype)
    hfin_ref[...] = h_s[...]


def _rw_scan(y1, y2, bonus, g, ac, cc, h0, lw, batch, seq, tm):
    n = batch * seq
    nt = seq // tm
    ncs = tm // CHUNK
    full = lambda shape: pl.BlockSpec(shape, lambda b, i: (0,) * len(shape))
    row_blk = pl.BlockSpec((tm, RW_WIDTH), lambda b, i: (b * nt + i, 0))
    mat_blk = pl.BlockSpec((ncs, LANES, RW_WIDTH), lambda b, i: (b * nt + i, 0, 0))
    st_blk = pl.BlockSpec((None, LANES, RW_WIDTH), lambda b, i: (b, 0, 0))
    return pl.pallas_call(
        functools.partial(_rw_scan_kernel, ncs=ncs),
        grid=(batch, nt),
        in_specs=[row_blk, row_blk, row_blk, row_blk, mat_blk, mat_blk, st_blk,
                  full((1, RW_WIDTH)), full((1, RW_WIDTH)), full((RW_WIDTH, RW_WIDTH))],
        out_specs=[row_blk, st_blk],
        out_shape=[jax.ShapeDtypeStruct((n, RW_WIDTH), BF16),
                   jax.ShapeDtypeStruct((batch, LANES, RW_WIDTH), F32)],
        scratch_shapes=[pltpu.VMEM((LANES, RW_WIDTH), F32)],
        compiler_params=_cp(2),
        name="rwkv_scan",
    )(y1, y2, bonus, g, ac, cc, h0, lw["gn_g"], lw["gn_b"], lw["segm"])


def _rw_step_prep_kernel(p_ref, pp_ref, mu_ref, w0_ref, wB_ref, a0_ref, aB_ref, gB_ref,
                         kk_ref, ka_ref, segm_ref, r_o, k_o, v_o, kk_o, b_o, w_o, g_o):
    r, k2, v, kk, b, ld, g = _rw_prep(p_ref[...], pp_ref[...], mu_ref[...], w0_ref[...], wB_ref[...],
                                      a0_ref[...], aB_ref[...], gB_ref[...], kk_ref[...], ka_ref[...],
                                      segm_ref[...])
    r_o[...] = r
    k_o[...] = k2
    v_o[...] = v
    kk_o[...] = kk
    b_o[...] = b
    w_o[...] = jnp.exp(ld)
    g_o[...] = g


def _rw_step_prep(p, p_prev, lw):
    n = p.shape[0]
    return pl.pallas_call(
        _rw_step_prep_kernel,
        out_shape=[jax.ShapeDtypeStruct((n, RW_WIDTH), F32)] * 7,
        compiler_params=pltpu.CompilerParams(vmem_limit_bytes=VMEM_LIMIT),
        name="rwkv_step_prep",
    )(p, p_prev, lw["mu"], lw["w0"], lw["wB2"], lw["a0"], lw["aB2"], lw["gB"], lw["kk"], lw["ka"], lw["segm"])


def _rw_step_kernel(s_ref, r_ref, w_ref, k_ref, kk_ref, b_ref, v_ref, g_ref, rk_ref, gng_ref, gnb_ref,
                    s_o, y_o):
    s = s_ref[...]
    r, w, k, kk, b = r_ref[...], w_ref[...], k_ref[...], kk_ref[...], b_ref[...]
    v = v_ref[...]
    rb = lambda a: a.astype(BF16).astype(F32)
    sa = jnp.sum(rb(s) * rb(-kk), axis=-1, keepdims=True)
    sn = s * w + sa * b + v * k
    s_o[...] = sn
    y = jnp.sum(rb(sn) * rb(r), axis=-1, keepdims=True)
    mu = jnp.mean(y, axis=1, keepdims=True)
    yc = y - mu
    var = jnp.mean(yc * yc, axis=1, keepdims=True)
    yn = yc * lax.rsqrt(var + RW_GN_EPS) * gng_ref[...] + gnb_ref[...]
    bonus = jnp.sum(r * k * rk_ref[...], axis=-1, keepdims=True) * v
    y_o[...] = (yn + bonus) * g_ref[...]


def _rw_step(s0, r, w, k, kk, b, v, g, lw):
    db = s0.shape[0]
    hk = lambda a: a.reshape(db, RW_HEADS, 1, RW_HEAD)
    hv = lambda a: a.reshape(db, RW_HEADS, RW_HEAD, 1)
    rowk = pl.BlockSpec((None, RW_HEADS, 1, RW_HEAD), lambda i: (i, 0, 0, 0))
    colv = pl.BlockSpec((None, RW_HEADS, RW_HEAD, 1), lambda i: (i, 0, 0, 0))
    st = pl.BlockSpec((None, RW_HEADS, RW_HEAD, RW_HEAD), lambda i: (i, 0, 0, 0))
    prk = pl.BlockSpec((RW_HEADS, 1, RW_HEAD), lambda i: (0, 0, 0))
    pcv = pl.BlockSpec((RW_HEADS, RW_HEAD, 1), lambda i: (0, 0, 0))
    s_new, y = pl.pallas_call(
        _rw_step_kernel,
        grid=(db,),
        in_specs=[st, rowk, rowk, rowk, rowk, rowk, colv, colv, prk, pcv, pcv],
        out_specs=[st, colv],
        out_shape=[jax.ShapeDtypeStruct(s0.shape, F32),
                   jax.ShapeDtypeStruct((db, RW_HEADS, RW_HEAD, 1), F32)],
        compiler_params=_cp(1),
        name="rwkv_step",
    )(s0, hk(r), hk(w), hk(k), hk(kk), hk(b), hv(v), hv(g),
      lw["rk"].reshape(RW_HEADS, 1, RW_HEAD), lw["gn_g"].reshape(RW_HEADS, RW_HEAD, 1),
      lw["gn_b"].reshape(RW_HEADS, RW_HEAD, 1))
    return s_new, y.reshape(db, RW_WIDTH)


def _sortable(x):
    bits = pltpu.bitcast(x, I32)
    return jnp.where(bits < 0, bits ^ jnp.int32(INT_MAX), bits)


def _dsa_prompt_kernel(iq_ref, iw_ref, q_ref, ik_ref, k_ref, v_ref, bias_ref, o_ref,
                       ksc, mbsc, wbsc, iqm, qmsc, tsc, jsc, *, topk):
    j = pl.program_id(1)
    nk = j + 1
    ones_bf = jnp.ones((LANES, LANES), BF16)
    row = lax.broadcasted_iota(I32, (QBLK, LANES), 0)
    col = lax.broadcasted_iota(I32, (QBLK, LANES), 1)
    qpos = j * QBLK + row
    wscale = IDX_HEADS ** -0.5 * IDX_HEAD ** -0.5

    iw = iw_ref[...]
    for h in range(IDX_HEADS):
        wbsc[h] = jnp.broadcast_to(iw[:, h:h + 1] * wscale, (QBLK, LANES))
        iqp = iq_ref[:, (h // 2) * LANES:(h // 2 + 1) * LANES]
        iqm[h] = jnp.where(_head_mask(h % 2), iqp, jnp.zeros_like(iqp))

    def score_chunk(c, carry):
        c0 = pl.multiple_of(c * KCH, KCH)
        ikc = ik_ref[pl.ds(c0, KCH), :]
        acc = jnp.zeros((QBLK, KCH), F32)
        for h in range(IDX_HEADS):
            s = _dot_nt(iqm[h], ikc)
            wb = wbsc[h]
            acc = acc + jnp.maximum(s, 0.0) * jnp.concatenate([wb] * (KCH // LANES), axis=1)
        key = _sortable(acc)
        for t in range(KCH // LANES):
            kb = c * (KCH // LANES) + t
            valid = (kb * LANES + col) <= qpos
            ksc[kb] = jnp.where(valid, key[:, t * LANES:(t + 1) * LANES], jnp.int32(INT_MIN))
        return carry

    lax.fori_loop(0, j // (KCH // LANES) + 1, score_chunk, 0)

    def lane_total(cnt):
        return jnp.dot(cnt.astype(BF16), ones_bf, preferred_element_type=F32)

    def count(pred):
        def body(kb, cnt):
            return cnt + pred(kb, ksc[kb]).astype(I32)
        return lane_total(lax.fori_loop(0, nk, body, jnp.zeros((QBLK, LANES), I32)))

    ktop = jnp.float32(topk)

    def bisect(_, lohi):
        lo, hi = lohi
        mid = lo + lax.shift_right_logical(hi - lo, 1)
        ok = count(lambda kb, key: key >= mid) >= ktop
        return jnp.where(ok, mid, lo), jnp.where(ok, hi, mid)

    lo, _ = lax.fori_loop(0, 32, bisect, (jnp.full((QBLK, LANES), INT_MIN, I32),
                                           jnp.full((QBLK, LANES), INT_MAX, I32)))
    thr = lo
    tsc[...] = thr
    jsc[...] = jnp.full((QBLK, LANES), INT_MAX, I32)
    c_ge = count(lambda kb, key: key >= thr)

    @pl.when(jnp.max(c_ge) > ktop)
    def _ties():
        need = ktop - count(lambda kb, key: key > thr)

        def bis_j(_, lohi):
            lo_j, hi_j = lohi
            mid = lax.shift_right_arithmetic(lo_j + hi_j, 1)
            ok = count(lambda kb, key: (key == thr) & ((kb * LANES + col) <= mid)) >= need
            return jnp.where(ok, lo_j, mid), jnp.where(ok, mid, hi_j)

        _, hi_j = lax.fori_loop(0, 15, bis_j, (jnp.full((QBLK, LANES), -1, I32),
                                               jnp.full((QBLK, LANES), 1, I32) * (nk * LANES - 1)))
        jsc[...] = hi_j

    thr = tsc[...]
    jlim = jsc[...]

    def mask_block(kb, carry):
        key = ksc[kb]
        lpos = kb * LANES + col
        sel = (key > thr) | ((key == thr) & (lpos <= jlim))
        mbsc[kb] = jnp.where(sel & (lpos <= qpos), 0.0, NEG)
        return carry

    lax.fori_loop(0, nk, mask_block, 0)

    nfull = jnp.maximum(j - 1, 0) // (KCH // LANES)
    pair = lambda h: slice((h // 2) * LANES, (h // 2 + 1) * LANES)
    for h in range(AT_HEADS):
        qp = q_ref[:, pair(h)]
        qmsc[h] = jnp.where(_head_mask(h % 2), qp, jnp.zeros_like(qp))

    def update(carry, s, vc):
        m, l, acc = carry
        mn = jnp.maximum(m, jnp.max(s, axis=1, keepdims=True))
        alpha = jnp.exp(m - mn)
        p = jnp.exp(s - mn)
        l = l * alpha + jnp.sum(p, axis=1, keepdims=True)
        acc = acc * alpha + jnp.dot(p.astype(BF16), vc, preferred_element_type=F32)
        return mn, l, acc

    def far(c, carry):
        c0 = pl.multiple_of(c * KCH, KCH)
        mb = jnp.concatenate([mbsc[c * (KCH // LANES) + t] for t in range(KCH // LANES)], axis=1)
        new = []
        for h in range(AT_HEADS):
            kc = k_ref[pl.ds(c0, KCH), pair(h)]
            vc = v_ref[pl.ds(c0, KCH), pair(h)]
            new.append(update(carry[h], _dot_nt(qmsc[h], kc) + mb, vc))
        return tuple(new)

    def near(kb, carry):
        k0 = pl.multiple_of(kb * LANES, LANES)
        wsel = jnp.clip(kb - j + 2, 0, 2)
        mb = mbsc[kb]
        new = []
        for h in range(AT_HEADS):
            kc = k_ref[pl.ds(k0, LANES), pair(h)]
            vc = v_ref[pl.ds(k0, LANES), pair(h)]
            new.append(update(carry[h], _dot_nt(qmsc[h], kc) + mb + bias_ref[wsel, h], vc))
        return tuple(new)

    init = tuple((jnp.full((QBLK, 1), NEG, F32), jnp.zeros((QBLK, 1), F32), jnp.zeros((QBLK, LANES), F32))
                 for _ in range(AT_HEADS))
    carry = lax.fori_loop(0, nfull, far, init)
    carry = lax.fori_loop(nfull * (KCH // LANES), nk, near, carry)
    outs = []
    for pr in range(AT_HEADS // 2):
        (_, l0, a0), (_, l1, a1) = carry[2 * pr], carry[2 * pr + 1]
        outs.append(jnp.where(_head_mask(0), a0 / l0, a1 / l1))
    o_ref[...] = jnp.concatenate(outs, axis=1).astype(o_ref.dtype)


def _dsa_prompt(iq_bf, iw, q_bf, ik2_bf, k_bf, v_bf, bias3, batch, seq):
    n = batch * seq
    nqb = seq // QBLK
    topk = min(TOPK_MAX, seq // 4)
    assert seq % KCH == 0
    qblk = lambda wd: pl.BlockSpec((QBLK, wd), lambda b, j: (b * nqb + j, 0))
    whole = lambda wd: pl.BlockSpec((seq, wd), lambda b, j: (b, 0), pipeline_mode=pl.Buffered(1))
    return pl.pallas_call(
        functools.partial(_dsa_prompt_kernel, topk=topk),
        grid=(batch, nqb),
        in_specs=[qblk(AT_WIDTH), qblk(IDX_HEADS), qblk(AT_WIDTH), whole(LANES), whole(AT_WIDTH), whole(AT_WIDTH),
                  pl.BlockSpec((3, AT_HEADS, QBLK, LANES), lambda b, j: (0, 0, 0, 0))],
        out_specs=qblk(AT_WIDTH),
        out_shape=jax.ShapeDtypeStruct((n, AT_WIDTH), BF16),
        scratch_shapes=[pltpu.VMEM((seq // LANES, QBLK, LANES), I32),
                        pltpu.VMEM((seq // LANES, QBLK, LANES), F32),
                        pltpu.VMEM((IDX_HEADS, QBLK, LANES), F32),
                        pltpu.VMEM((IDX_HEADS, QBLK, LANES), BF16),
                        pltpu.VMEM((AT_HEADS, QBLK, LANES), BF16),
                        pltpu.VMEM((QBLK, LANES), I32),
                        pltpu.VMEM((QBLK, LANES), I32)],
        compiler_params=_cp(2),
        name="dsa_prompt",
    )(iq_bf, iw, q_bf, ik2_bf, k_bf, v_bf, bias3)


def _dsa_sel_kernel(pt_ref, iq_ref, w_ref, ikn_ref, cik_hbm, mb_ref, ikbuf, sem, *, layer, past, topk):
    b = pl.program_id(0)
    n_pages = past // PAGE
    lpad = past + PAGE

    def page_copy(pg):
        return pltpu.make_async_copy(cik_hbm.at[layer, pt_ref[b, pg]],
                                     ikbuf.at[pl.ds(pl.multiple_of(pg * PAGE, PAGE), PAGE), :], sem)

    def issue(pg, c):
        page_copy(pg).start()
        return c

    lax.fori_loop(0, n_pages, issue, 0)
    ikbuf[pl.ds(past, PAGE), :] = jnp.zeros((PAGE, IDX_HEAD), F32)
    ikbuf[pl.ds(past, 1), :] = ikn_ref[...]

    def wait(pg, c):
        page_copy(pg).wait()
        return c

    lax.fori_loop(0, n_pages, wait, 0)

    iq = iq_ref[...]
    w = (w_ref[...] * (IDX_HEADS ** -0.5)).astype(BF16).astype(F32)
    pieces = []
    step = 2048
    for c0 in list(range(0, past, step)) + [past]:
        wd = min(step, past - c0) if c0 < past else PAGE
        s = _dot_nt(iq, ikbuf[pl.ds(c0, wd), :])
        rl = (jnp.maximum(s, 0.0) * (IDX_HEAD ** -0.5)).astype(BF16).astype(F32)
        sc = jnp.sum(rl * w, axis=0, keepdims=True) + 0.0
        pieces.append(sc)
    score = jnp.concatenate(pieces, axis=1)
    pos = lax.broadcasted_iota(I32, (1, lpad), 1)
    key = jnp.where(pos <= past, _sortable(score), jnp.int32(INT_MIN))
    ktop = jnp.float32(topk)

    def count(pred):
        return jnp.sum(pred.astype(F32), axis=1, keepdims=True)

    def bisect(_, lohi):
        lo, hi = lohi
        mid = lo + lax.shift_right_logical(hi - lo, 1)
        ok = count(key >= mid) >= ktop
        return jnp.where(ok, mid, lo), jnp.where(ok, hi, mid)

    thr, _ = lax.fori_loop(0, 32, bisect, (jnp.full((1, 1), INT_MIN, I32), jnp.full((1, 1), INT_MAX, I32)))
    need = ktop - count(key > thr)

    def bis_j(_, lohi):
        lo_j, hi_j = lohi
        mid = lax.shift_right_arithmetic(lo_j + hi_j, 1)
        ok = count((key == thr) & (pos <= mid)) >= need
        return jnp.where(ok, lo_j, mid), jnp.where(ok, mid, hi_j)

    _, jlim = lax.fori_loop(0, 16, bis_j, (jnp.full((1, 1), -1, I32), jnp.full((1, 1), lpad - 1, I32)))
    sel = (key > thr) | ((key == thr) & (pos <= jlim))
    mb_ref[...] = jnp.where(sel & (pos <= past), 0.0, NEG)


def _dsa_sel(page_table, iq_bf, iw, ik_new, cache_idx_k, layer, past):
    db = iq_bf.shape[0]
    lpad = past + PAGE
    topk = min(TOPK_MAX, (past + 1) // 4)
    gs = pltpu.PrefetchScalarGridSpec(
        num_scalar_prefetch=1,
        grid=(db,),
        in_specs=[pl.BlockSpec((None, IDX_HEADS, IDX_HEAD), lambda b, pt: (b, 0, 0)),
                  pl.BlockSpec((None, IDX_HEADS, 1), lambda b, pt: (b, 0, 0)),
                  pl.BlockSpec((None, 1, IDX_HEAD), lambda b, pt: (b, 0, 0)),
                  pl.BlockSpec(memory_space=pl.ANY)],
        out_specs=pl.BlockSpec((None, 1, lpad), lambda b, pt: (b, 0, 0)),
        scratch_shapes=[pltpu.VMEM((lpad, IDX_HEAD), F32), pltpu.SemaphoreType.DMA(())],
    )
    return pl.pallas_call(
        functools.partial(_dsa_sel_kernel, layer=layer, past=past, topk=topk),
        grid_spec=gs,
        out_shape=jax.ShapeDtypeStruct((db, 1, lpad), F32),
        compiler_params=_cp(1),
        name="dsa_sample_select",
    )(page_table, iq_bf.reshape(db, IDX_HEADS, IDX_HEAD), iw.reshape(db, IDX_HEADS, 1),
      ik_new.reshape(db, 1, IDX_HEAD), cache_idx_k)


def _dsa_att_kernel(pt_ref, q_ref, k_ref, v_ref, mb_ref, bl_ref, kn_ref, vn_ref, mbn_ref, bn_ref,
                    segm_ref, o_ref, m_s, l_s, acc_s):
    p = pl.program_id(1)
    last = p == pl.num_programs(1) - 1
    segm = segm_ref[...]

    @pl.when(p == 0)
    def _():
        m_s[...] = jnp.full(m_s.shape, NEG, F32)
        l_s[...] = jnp.zeros(l_s.shape, F32)
        acc_s[...] = jnp.zeros(acc_s.shape, F32)

    q = q_ref[...] * (AT_HEAD ** -0.5)

    def accumulate(kp, vp, bias, mb):
        lg = _dot_x(kp * q, segm) + bias + mb
        m = m_s[...]
        mn = jnp.maximum(m, jnp.max(lg, axis=0, keepdims=True))
        alpha = jnp.exp(m - mn)
        pr = jnp.exp(lg - mn)
        l_s[...] = l_s[...] * alpha + jnp.sum(pr, axis=0, keepdims=True)
        acc_s[...] = acc_s[...] * alpha + jnp.sum(pr * vp, axis=0, keepdims=True)
        m_s[...] = mn

    accumulate(k_ref[...], v_ref[...], jnp.where(last, bl_ref[...], 0.0), mb_ref[...])

    @pl.when(last)
    def _():
        accumulate(kn_ref[...], vn_ref[...], bn_ref[...], mbn_ref[...])
        o_ref[...] = (acc_s[...] / l_s[...]).astype(o_ref.dtype)


def _dsa_att(page_table, q, k_new, v_new, mb, bias_last, bias_new, cache_k, cache_v, layer, past, segm):
    db = q.shape[0]
    n_pages = past // PAGE
    n_pool = cache_k.shape[1]
    ck = cache_k.reshape(cache_k.shape[0], n_pool, PAGE, AT_WIDTH)
    cv = cache_v.reshape(cache_v.shape[0], n_pool, PAGE, AT_WIDTH)
    mb_col = mb.reshape(db, past + PAGE, 1)
    row = lambda wd: pl.BlockSpec((None, 1, wd), lambda b, p, pt: (b, 0, 0))
    page = pl.BlockSpec((None, None, PAGE, AT_WIDTH), lambda b, p, pt: (layer, pt[b, p], 0, 0))
    gs = pltpu.PrefetchScalarGridSpec(
        num_scalar_prefetch=1,
        grid=(db, n_pages),
        in_specs=[row(AT_WIDTH), page, page,
                  pl.BlockSpec((None, PAGE, 1), lambda b, p, pt: (b, p, 0)),
                  pl.BlockSpec((PAGE, AT_WIDTH), lambda b, p, pt: (0, 0)),
                  row(AT_WIDTH), row(AT_WIDTH),
                  pl.BlockSpec((None, 1, 1), lambda b, p, pt: (b, 0, 0)),
                  pl.BlockSpec((1, AT_WIDTH), lambda b, p, pt: (0, 0)),
                  pl.BlockSpec((AT_WIDTH, AT_WIDTH), lambda b, p, pt: (0, 0))],
        out_specs=row(AT_WIDTH),
        scratch_shapes=[pltpu.VMEM((1, AT_WIDTH), F32), pltpu.VMEM((1, AT_WIDTH), F32),
                        pltpu.VMEM((1, AT_WIDTH), F32)],
    )
    out = pl.pallas_call(
        _dsa_att_kernel,
        grid_spec=gs,
        out_shape=jax.ShapeDtypeStruct((db, 1, AT_WIDTH), BF16),
        compiler_params=_cp(2),
        name="dsa_sample_attend",
    )(page_table, q.reshape(db, 1, AT_WIDTH), ck, cv, mb_col, bias_last,
      k_new.reshape(db, 1, AT_WIDTH), v_new.reshape(db, 1, AT_WIDTH),
      mb_col[:, past:past + 1, :], bias_new, segm)
    return out.reshape(db, AT_WIDTH)


def _mem_kernel(x_ref, w_ref, mk_ref, mv_ref, o_ref):
    mq = jnp.dot(x_ref[...].astype(BF16), w_ref[...], preferred_element_type=F32)
    outs = []
    for h in range(MEM_HEADS):
        ln = slice(h * MEM_HEAD, (h + 1) * MEM_HEAD)
        s = _dot_nt(mq[:, ln], mk_ref[:, ln]) * (MEM_HEAD ** -0.5)
        s = s - jnp.max(s, axis=1, keepdims=True)
        p = jnp.exp(s)
        p = p / jnp.sum(p, axis=1, keepdims=True)
        outs.append(_dot(p, mv_ref[:, ln]))
    o_ref[...] = jnp.concatenate(outs, axis=1).astype(o_ref.dtype)


def _mem_attend(x3, w_mq_bf, mk, mv, tm):
    bm, t, _ = x3.shape
    tm = min(tm, t)
    mlen = mk.shape[1]
    return pl.pallas_call(
        _mem_kernel,
        grid=(bm, t // tm),
        in_specs=[pl.BlockSpec((None, tm, D_MODEL), lambda b, i: (b, i, 0)),
                  pl.BlockSpec((D_MODEL, MEM_WIDTH), lambda b, i: (0, 0)),
                  pl.BlockSpec((None, mlen, MEM_WIDTH), lambda b, i: (b, 0, 0)),
                  pl.BlockSpec((None, mlen, MEM_WIDTH), lambda b, i: (b, 0, 0))],
        out_specs=pl.BlockSpec((None, tm, MEM_WIDTH), lambda b, i: (b, i, 0)),
        out_shape=jax.ShapeDtypeStruct((bm, t, MEM_WIDTH), BF16),
        compiler_params=_cp(2),
        name="mem_attend",
    )(x3, w_mq_bf, mk, mv)


def _merge_kernel(x_ref, yrw_ref, yat_ref, ymem_ref, wg_ref, wrw_ref, wat_ref, wmem_ref, wout_ref,
                  g1_ref, b1_ref, rw_ref, rb_ref, x1_ref, ti_ref, tg_ref, *, alpha):
    x = x_ref[...]
    gates = jax.nn.sigmoid(jnp.dot(x.astype(BF16), wg_ref[...], preferred_element_type=F32))
    d = D_MODEL
    merged = (gates[:, 0:d] * jnp.dot(yrw_ref[...], wrw_ref[...], preferred_element_type=F32)
              + gates[:, d:2 * d] * jnp.dot(yat_ref[...], wat_ref[...], preferred_element_type=F32)
              + gates[:, 2 * d:3 * d] * jnp.dot(ymem_ref[...], wmem_ref[...], preferred_element_type=F32))
    mix = jnp.dot(merged.astype(BF16), wout_ref[...], preferred_element_type=F32)
    x1 = _layer_norm(alpha * x + mix, g1_ref[...], b1_ref[...])
    x1_ref[...] = x1
    logits = _dot(x1, rw_ref[...]) + rb_ref[...]
    lane = lax.broadcasted_iota(I32, logits.shape, 1)
    idx_out = jnp.zeros(logits.shape, I32)
    val_out = jnp.full(logits.shape, NEG, F32)
    for r in range(TOP_K):
        m = jnp.max(logits, axis=1, keepdims=True)
        am = jnp.min(jnp.where(logits == m, lane, LANES), axis=1, keepdims=True)
        idx_out = jnp.where(lane == r, am, idx_out)
        val_out = jnp.where(lane == r, m, val_out)
        logits = jnp.where(lane == am, -jnp.inf, logits)
    e = jnp.exp(val_out - jnp.max(val_out, axis=1, keepdims=True))
    tg_ref[...] = e / jnp.sum(e, axis=1, keepdims=True)
    ti_ref[...] = idx_out


def _merge(x, y_rw, y_at, y_mem, lw, tm, alpha):
    n = x.shape[0]
    tm = min(tm, n)
    full = lambda a: pl.BlockSpec(a.shape, lambda i: (0,) * a.ndim)
    rowb = lambda wd: pl.BlockSpec((tm, wd), lambda i: (i, 0))
    ws = [lw["w_gate"], lw["w_br_rw"], lw["w_br_at"], lw["w_br_mem"], lw["w_out"],
          lw["ln1_g"], lw["ln1_b"], lw["router_w"], lw["router_b"]]
    return pl.pallas_call(
        functools.partial(_merge_kernel, alpha=alpha),
        grid=(n // tm,),
        in_specs=[rowb(D_MODEL), rowb(RW_WIDTH), rowb(AT_WIDTH), rowb(MEM_WIDTH)] + [full(a) for a in ws],
        out_specs=[rowb(D_MODEL), rowb(LANES), rowb(LANES)],
        out_shape=[jax.ShapeDtypeStruct((n, D_MODEL), F32), jax.ShapeDtypeStruct((n, LANES), I32),
                   jax.ShapeDtypeStruct((n, LANES), F32)],
        compiler_params=_cp(1),
        name="merge_ln_router",
    )(x, y_rw, y_at, y_mem, *ws)


def _ffn(xb, wg, bg, wu, bu, wd, bd):
    hg = jnp.minimum(jnp.dot(xb, wg.astype(BF16), preferred_element_type=F32) + bg, SWIGLU_LIMIT)
    hl = jnp.clip(jnp.dot(xb, wu.astype(BF16), preferred_element_type=F32) + bu, -SWIGLU_LIMIT, SWIGLU_LIMIT)
    h = hg * jax.nn.sigmoid(SWIGLU_ALPHA * hg) * (hl + 1.0)
    return jnp.dot(h.astype(BF16), wd.astype(BF16), preferred_element_type=F32) + bd


def _moe_expert_kernel(be_ref, nu_ref, tok_ref, x_hbm, gate_ref, wg_ref, bg_ref, wu_ref, bu_ref, wd_ref, bd_ref,
                       o_ref, xbuf, sem, *, blk):
    i = pl.program_id(0)

    def row_copy(r, tok):
        return pltpu.make_async_copy(x_hbm.at[pl.ds(tok, 1), :], xbuf.at[pl.ds(r, 1), :], sem)

    @pl.when(i < nu_ref[0])
    def _():
        def issue(r, c):
            row_copy(r, tok_ref[0, 0, r]).start()
            return c

        lax.fori_loop(0, blk, issue, 0, unroll=8)

        def wait(r, c):
            row_copy(r, 0).wait()
            return c

        lax.fori_loop(0, blk, wait, 0, unroll=8)
        out = _ffn(xbuf[...].astype(BF16), wg_ref[...], bg_ref[...], wu_ref[...], bu_ref[...],
                   wd_ref[...], bd_ref[...])
        o_ref[...] = out * gate_ref[...]

    @pl.when(i >= nu_ref[0])
    def _():
        o_ref[...] = jnp.zeros(o_ref.shape, F32)


def _moe_experts(blk_exp, n_used, buf_tok, x1, buf_gate, moe, layer, blk):
    n_blocks = blk_exp.shape[0]
    cap = n_blocks * blk
    wspec = pl.BlockSpec((None, None, D_MODEL, D_FF), lambda i, be, nu: (layer, be[i], 0, 0))
    wdspec = pl.BlockSpec((None, None, D_FF, D_MODEL), lambda i, be, nu: (layer, be[i], 0, 0))
    bspec = pl.BlockSpec((None, None, 1, D_FF), lambda i, be, nu: (layer, be[i], 0, 0))
    gs = pltpu.PrefetchScalarGridSpec(
        num_scalar_prefetch=2,
        grid=(n_blocks,),
        in_specs=[pl.BlockSpec((1, 1, blk), lambda i, be, nu: (i, 0, 0), memory_space=pltpu.SMEM),
                  pl.BlockSpec(memory_space=pl.ANY),
                  pl.BlockSpec((blk, 1), lambda i, be, nu: (i, 0)),
                  wspec, bspec, wspec, bspec, wdspec, bspec],
        out_specs=pl.BlockSpec((blk, D_MODEL), lambda i, be, nu: (i, 0)),
        scratch_shapes=[pltpu.VMEM((blk, D_MODEL), F32), pltpu.SemaphoreType.DMA(())],
    )
    e = N_EXPERTS
    depth = moe["w_gate"].shape[0]
    b4 = lambda a: a.reshape(depth, e, 1, a.shape[-1])
    return pl.pallas_call(
        functools.partial(_moe_expert_kernel, blk=blk),
        grid_spec=gs,
        out_shape=jax.ShapeDtypeStruct((cap, D_MODEL), F32),
        compiler_params=_cp(1),
        name="moe_experts",
    )(blk_exp, n_used, buf_tok.reshape(n_blocks, 1, blk), x1, buf_gate.reshape(cap, 1),
      moe["w_gate"], b4(moe["b_gate"]), moe["w_up"], b4(moe["b_up"]), moe["w_down"], b4(moe["b_down"]))


def _moe_combine_kernel(dest_ref, o_hbm, x1_ref, g2_ref, b2_ref, y_ref, buf, sem, *, tmc, alpha):
    def row_copy(jj, t, d):
        return pltpu.make_async_copy(o_hbm.at[pl.ds(d, 1), :], buf.at[jj, pl.ds(t, 1), :], sem)

    for jj in range(TOP_K):
        def issue(t, c):
            row_copy(jj, t, dest_ref[0, jj, t]).start()
            return c

        lax.fori_loop(0, tmc, issue, 0, unroll=8)
    for jj in range(TOP_K):
        def wait(t, c):
            row_copy(jj, t, 0).wait()
            return c

        lax.fori_loop(0, tmc, wait, 0, unroll=8)
    y = (buf[0] + buf[1]) + (buf[2] + buf[3])
    y_ref[...] = _layer_norm(alpha * x1_ref[...] + y, g2_ref[...], b2_ref[...])


def _moe_combine(dest, out_sorted, x1, ln2_g, ln2_b, tmc, alpha):
    n = x1.shape[0]
    nt = n // tmc
    dest3 = dest.reshape(nt, tmc, TOP_K).transpose(0, 2, 1)
    return pl.pallas_call(
        functools.partial(_moe_combine_kernel, tmc=tmc, alpha=alpha),
        grid=(nt,),
        in_specs=[pl.BlockSpec((1, TOP_K, tmc), lambda i: (i, 0, 0), memory_space=pltpu.SMEM),
                  pl.BlockSpec(memory_space=pl.ANY),
                  pl.BlockSpec((tmc, D_MODEL), lambda i: (i, 0)),
                  pl.BlockSpec((1, D_MODEL), lambda i: (0, 0)),
                  pl.BlockSpec((1, D_MODEL), lambda i: (0, 0))],
        out_specs=pl.BlockSpec((tmc, D_MODEL), lambda i: (i, 0)),
        out_shape=jax.ShapeDtypeStruct((n, D_MODEL), F32),
        scratch_shapes=[pltpu.VMEM((TOP_K, tmc, D_MODEL), F32), pltpu.SemaphoreType.DMA(())],
        compiler_params=_cp(1),
        name="moe_combine_ln",
    )(dest3, out_sorted, x1, ln2_g, ln2_b)


def _moe_prompt(x1, top_idx, top_gate, moe, layer, ln2_g, ln2_b, alpha, blk, tmc):
    n = x1.shape[0]
    n_asg = n * TOP_K
    e = N_EXPERTS
    n_blocks = n_asg // blk + e
    cap = n_blocks * blk
    expert = top_idx.reshape(n_asg)
    token = jnp.repeat(jnp.arange(n, dtype=I32), TOP_K)
    order = jnp.argsort(expert, stable=True)
    e_s = expert[order]
    counts = jnp.bincount(expert, length=e)
    starts = jnp.cumsum(counts) - counts
    padded = (counts + blk - 1) // blk * blk
    pends = jnp.cumsum(padded)
    pstarts = pends - padded
    dest_sorted = (pstarts[e_s] + jnp.arange(n_asg) - starts[e_s]).astype(I32)
    buf_tok = jnp.zeros((cap,), I32).at[dest_sorted].set(token[order])
    buf_gate = jnp.zeros((cap,), F32).at[dest_sorted].set(top_gate.reshape(n_asg)[order])
    blk_exp = jnp.minimum(jnp.searchsorted(pends, jnp.arange(n_blocks) * blk, side="right"), e - 1).astype(I32)
    n_used = (pends[-1] // blk).astype(I32).reshape(1)
    dest = jnp.zeros((n_asg,), I32).at[order].set(dest_sorted)
    out_sorted = _moe_experts(blk_exp, n_used, buf_tok, x1, buf_gate, moe, layer, blk)
    return _moe_combine(dest, out_sorted, x1, ln2_g, ln2_b, tmc, alpha)


def _moe_dense_kernel(x_ref, ti_ref, tg_ref, wg_ref, bg_ref, wu_ref, bu_ref, wd_ref, bd_ref, g2_ref, b2_ref,
                      y_ref, acc, *, alpha):
    e = pl.program_id(0)

    @pl.when(e == 0)
    def _():
        acc[...] = jnp.zeros(acc.shape, F32)

    x = x_ref[...]
    ti = ti_ref[...]
    tg = tg_ref[...]
    lane = lax.broadcasted_iota(I32, ti.shape, 1)
    cw = jnp.sum(jnp.where((ti == e) & (lane < TOP_K), tg, 0.0), axis=1, keepdims=True)
    out = _ffn(x.astype(BF16), wg_ref[...], bg_ref[...], wu_ref[...], bu_ref[...], wd_ref[...], bd_ref[...])
    acc[...] = acc[...] + cw * out

    @pl.when(e == pl.num_programs(0) - 1)
    def _():
        y_ref[...] = _layer_norm(alpha * x + acc[...], g2_ref[...], b2_ref[...])


def _moe_dense(x1, top_idx, top_gate, moe, layer, ln2_g, ln2_b, alpha):
    n = x1.shape[0]
    e = N_EXPERTS
    depth = moe["w_gate"].shape[0]
    b4 = lambda a: a.reshape(depth, e, 1, a.shape[-1])
    full = lambda a: pl.BlockSpec(a.shape, lambda i: (0,) * a.ndim)
    wspec = pl.BlockSpec((None, None, D_MODEL, D_FF), lambda i: (layer, i, 0, 0))
    wdspec = pl.BlockSpec((None, None, D_FF, D_MODEL), lambda i: (layer, i, 0, 0))
    bspec = pl.BlockSpec((None, None, 1, D_FF), lambda i: (layer, i, 0, 0))
    return pl.pallas_call(
        functools.partial(_moe_dense_kernel, alpha=alpha),
        grid=(e,),
        in_specs=[full(x1), full(top_idx), full(top_gate), wspec, bspec, wspec, bspec, wdspec, bspec,
                  full(ln2_g), full(ln2_b)],
        out_specs=pl.BlockSpec((n, D_MODEL), lambda i: (0, 0)),
        out_shape=jax.ShapeDtypeStruct((n, D_MODEL), F32),
        scratch_shapes=[pltpu.VMEM((n, D_MODEL), F32)],
        compiler_params=_cp(1),
        name="moe_dense",
    )(x1, top_idx, top_gate, moe["w_gate"], b4(moe["b_gate"]), moe["w_up"], b4(moe["b_up"]),
      moe["w_down"], b4(moe["b_down"]), ln2_g, ln2_b)


def _t5_bucket(rel):
    n = jnp.maximum(rel, 0)
    max_exact = N_BUCKETS // 2
    nf = jnp.maximum(n, 1).astype(F32)
    large = max_exact + (jnp.log(nf / max_exact) / math.log(MAX_DISTANCE / max_exact)
                         * (N_BUCKETS - max_exact)).astype(I32)
    large = jnp.minimum(large, N_BUCKETS - 1)
    return jnp.where(n < max_exact, n, large)


def _const_mats():
    hid = np.arange(RW_WIDTH) // RW_HEAD
    segm = (hid[:, None] == hid[None, :]).astype(np.float32)
    ltri = np.tril(np.ones((CHUNK, CHUNK), np.float32))
    return jnp.asarray(segm, BF16), jnp.asarray(ltri, BF16)


def _layer_weights(l, w_in, rw_mu, rw_w0, rw_wB, rw_a0, rw_aB, rw_gB, rw_kk, rw_ka, rw_rk, rw_gn_g, rw_gn_b,
                   mem_wk, mem_wv, w_br_rw, w_br_at, w_br_mem, w_out, ln1_g, ln1_b, ln2_g, ln2_b,
                   router_w, router_b, segm, ltri):
    wl = w_in[l]
    o_at = RW_IN
    o_mq = RW_IN + AT_IN
    o_g = o_mq + MEM_WIDTH
    w_at = wl[:, o_at:o_at + AT_IN]
    ikc = w_at[:, 4 * AT_WIDTH:4 * AT_WIDTH + IDX_HEAD]
    iwc = w_at[:, 4 * AT_WIDTH + IDX_HEAD:]
    w_at2 = jnp.concatenate([w_at[:, :4 * AT_WIDTH], ikc, ikc, iwc,
                             jnp.zeros((D_MODEL, LANES - IDX_HEADS), F32)], axis=1)
    row = lambda a: a[l].reshape(1, -1)
    zpad = jnp.zeros((RW_LORA_W, RW_WIDTH), F32)
    return {
        "w_rw": wl[:, :RW_IN].astype(BF16),
        "w_at": w_at2.astype(BF16),
        "w_mq": wl[:, o_mq:o_g].astype(BF16),
        "w_gate": wl[:, o_g:].astype(BF16),
        "w_mem": jnp.concatenate([mem_wk[l], mem_wv[l]], axis=1).astype(BF16),
        "mu": row(rw_mu), "w0": row(rw_w0), "a0": row(rw_a0), "kk": row(rw_kk), "ka": row(rw_ka),
        "rk": row(rw_rk), "gn_g": row(rw_gn_g), "gn_b": row(rw_gn_b),
        "wB2": jnp.concatenate([rw_wB[l], zpad], axis=0).astype(BF16),
        "aB2": jnp.concatenate([zpad, rw_aB[l]], axis=0).astype(BF16),
        "gB": rw_gB[l].astype(BF16),
        "segm": segm, "ltri": ltri,
        "w_br_rw": w_br_rw[l].astype(BF16), "w_br_at": w_br_at[l].astype(BF16),
        "w_br_mem": w_br_mem[l].astype(BF16), "w_out": w_out[l].astype(BF16),
        "ln1_g": row(ln1_g), "ln1_b": row(ln1_b), "ln2_g": row(ln2_g), "ln2_b": row(ln2_b),
        "router_w": jnp.concatenate([router_w[l], jnp.zeros((D_MODEL, LANES - N_EXPERTS), F32)], axis=1),
        "router_b": jnp.concatenate([router_b[l], jnp.full((LANES - N_EXPERTS,), NEG, F32)]).reshape(1, LANES),
    }


_AT_COLS = [(0, 512, AT_HEAD ** -0.5), (512, 512, 1.0), (1024, 512, 1.0), (512, 512, 1.0), (1024, 512, 1.0),
            (1536, 512, 1.0), (2048, 64, 1.0), (2048, 128, 1.0), (2176, 8, 1.0)]
_AT_DTYPES = [BF16, F32, F32, BF16, BF16, BF16, F32, BF16, F32]


def kernel(x_prompt, x_sample, mem_prompt, cache_k, cache_v, cache_idx_k, page_table, cache_mem_k, cache_mem_v, state_rwkv, state_shift, w_in, rw_mu, rw_w0, rw_wB, rw_a0, rw_aB, rw_gB, rw_kk, rw_ka, rw_rk, rw_gn_g, rw_gn_b, mem_wk, mem_wv, rel_bias, w_br_rw, w_br_at, w_br_mem, w_out, ln1_g, ln1_b, ln2_g, ln2_b, router_w, router_b, moe_w_gate, moe_b_gate, moe_w_up, moe_b_up, moe_w_down, moe_b_down):
    depth = w_in.shape[0]
    bp, seq, _ = x_prompt.shape
    db = x_sample.shape[0]
    n = bp * seq
    mlen = mem_prompt.shape[1]
    past = page_table.shape[1] * PAGE
    alpha = (2 * depth) ** 0.25
    segm, ltri = _const_mats()
    moe = {"w_gate": moe_w_gate, "b_gate": moe_b_gate, "w_up": moe_w_up, "b_up": moe_b_up,
           "w_down": moe_w_down, "b_down": moe_b_down}

    bias_d = rel_bias[_t5_bucket(jnp.arange(2 * QBLK))]
    far = rel_bias[N_BUCKETS - 1]
    ri = np.arange(QBLK)[:, None]
    ci = np.arange(LANES)[None, :]
    prev_tile = (bias_d[QBLK + ri - ci] - far).transpose(2, 0, 1)
    diag_tile = (bias_d[np.maximum(ri - ci, 0)] - far).transpose(2, 0, 1)
    bias3 = jnp.stack([jnp.zeros_like(prev_tile), prev_tile, diag_tile])
    near_s = rel_bias[_t5_bucket(past - jnp.arange(past - PAGE, past + 1))] - far
    near_s = jnp.repeat(near_s, AT_HEAD, axis=1)
    bias_last, bias_new = near_s[:PAGE], near_s[PAGE:]

    xp = x_prompt.reshape(n, D_MODEL)
    xs = x_sample.reshape(db, D_MODEL)
    tm_rw = min(256, seq)
    acc = {k: [] for k in ("kp", "vp", "ikp", "mkp", "mvp", "sp", "shp", "ks", "vs", "iks", "ss", "shs")}

    for l in range(depth):
        lw = _layer_weights(l, w_in, rw_mu, rw_w0, rw_wB, rw_a0, rw_aB, rw_gB, rw_kk, rw_ka, rw_rk, rw_gn_g,
                            rw_gn_b, mem_wk, mem_wv, w_br_rw, w_br_at, w_br_mem, w_out, ln1_g, ln1_b,
                            ln2_g, ln2_b, router_w, router_b, segm, ltri)

        acc["shp"].append(xp.reshape(bp, seq, D_MODEL)[:, -1])
        mk, mv = _proj(mem_prompt.reshape(bp * mlen, D_MODEL), lw["w_mem"],
                       [(0, MEM_WIDTH, 1.0), (MEM_WIDTH, MEM_WIDTH, 1.0)], [F32, F32], 256)
        acc["mkp"].append(mk.reshape(bp, mlen, MEM_HEADS, MEM_HEAD))
        acc["mvp"].append(mv.reshape(bp, mlen, MEM_HEADS, MEM_HEAD))
        (p_rw,) = _proj(xp, lw["w_rw"], [(0, RW_IN, 1.0)], [F32], 512)
        q_bf, k32, v32, k_bf, v_bf, iq_bf, ik32, ik2_bf, iw = _proj(xp, lw["w_at"], _AT_COLS, _AT_DTYPES, 512)
        acc["kp"].append(k32.reshape(bp, seq, AT_HEADS, AT_HEAD))
        acc["vp"].append(v32.reshape(bp, seq, AT_HEADS, AT_HEAD))
        acc["ikp"].append(ik32.reshape(bp, seq, IDX_HEAD))

        y1, y2, bonus, g, ac, cc = _rw_chunks(p_rw, jnp.zeros((bp, 1, RW_IN), F32), lw, bp, seq, tm_rw)
        y_rw, hfin = _rw_scan(y1, y2, bonus, g, ac, cc, jnp.zeros((bp, LANES, RW_WIDTH), F32), lw, bp, seq, tm_rw)
        hf = hfin.reshape(bp, 2, RW_HEAD, RW_HEADS // 2, 2, RW_HEAD)
        s_fin = jnp.stack([hf[:, hh, :, :, hh, :] for hh in range(2)], axis=3)
        acc["sp"].append(s_fin.reshape(bp, RW_HEAD, RW_HEADS, RW_HEAD).transpose(0, 2, 3, 1))

        y_at = _dsa_prompt(iq_bf, iw, q_bf, ik2_bf, k_bf, v_bf, bias3, bp, seq)
        y_mem = _mem_attend(xp.reshape(bp, seq, D_MODEL), lw["w_mq"], mk.reshape(bp, mlen, MEM_WIDTH),
                            mv.reshape(bp, mlen, MEM_WIDTH), 512).reshape(n, MEM_WIDTH)
        x1, ti, tg = _merge(xp, y_rw, y_at, y_mem, lw, 256, alpha)
        xp = _moe_prompt(x1, ti[:, :TOP_K], tg[:, :TOP_K], moe, l, lw["ln2_g"], lw["ln2_b"], alpha, 256, 128)

        acc["shs"].append(xs)
        (p2,) = _proj(jnp.concatenate([xs, state_shift[l]], axis=0), lw["w_rw"], [(0, RW_IN, 1.0)], [F32], 2 * db)
        qs, ks32, vs32, _, _, iqs_bf, iks32, _, iws = _proj(xs, lw["w_at"], [(0, 512, 1.0)] + _AT_COLS[1:],
                                                            [F32] + _AT_DTYPES[1:], db)
        acc["ks"].append(ks32.reshape(db, 1, AT_HEADS, AT_HEAD))
        acc["vs"].append(vs32.reshape(db, 1, AT_HEADS, AT_HEAD))
        acc["iks"].append(iks32.reshape(db, 1, IDX_HEAD))
        r_, k_, v_, kk_, b_, w_, g_ = _rw_step_prep(p2[:db], p2[db:], lw)
        s_new, y_rw_s = _rw_step(state_rwkv[l], r_, w_, k_, kk_, b_, v_, g_, lw)
        acc["ss"].append(s_new)
        mb = _dsa_sel(page_table, iqs_bf, iws, iks32, cache_idx_k, l, past)
        y_at_s = _dsa_att(page_table, qs, ks32, vs32, mb, bias_last, bias_new, cache_k, cache_v, l, past, segm)
        y_mem_s = _mem_attend(xs.reshape(db, 1, D_MODEL), lw["w_mq"], cache_mem_k[l].reshape(db, mlen, MEM_WIDTH),
                              cache_mem_v[l].reshape(db, mlen, MEM_WIDTH), 1).reshape(db, MEM_WIDTH)
        x1s, tis, tgs = _merge(xs, y_rw_s.astype(BF16), y_at_s, y_mem_s, lw, db, alpha)
        xs = _moe_dense(x1s, tis, tgs, moe, l, lw["ln2_g"], lw["ln2_b"], alpha)

    st = lambda k: jnp.stack(acc[k])
    return (xp.reshape(bp, seq, D_MODEL), xs.reshape(db, 1, D_MODEL),
            st("kp"), st("vp"), st("ikp"), st("mkp"), st("mvp"), st("sp"), st("shp"),
            st("ks"), st("vs"), st("iks"), st("ss"), st("shs"))
```

```python
import functools
import math

import numpy as np
import jax
import jax.numpy as jnp
from jax import lax
from jax.experimental import pallas as pl
from jax.experimental.pallas import tpu as pltpu

F32 = jnp.float32
BF16 = jnp.bfloat16
I32 = jnp.int32

D_MODEL = 1024
RW_HEADS, RW_HEAD, RW_WIDTH = 8, 64, 512
RW_LORA_W, RW_LORA_A, RW_LORA_G = 64, 64, 128
RW_IN = 3 * RW_WIDTH + RW_LORA_W + RW_LORA_A + RW_LORA_G
RW_GN_EPS = 64e-5
AT_HEADS, AT_HEAD, AT_WIDTH = 8, 64, 512
IDX_HEADS, IDX_HEAD = 8, 64
AT_IN = 3 * AT_WIDTH + IDX_HEADS * IDX_HEAD + IDX_HEAD + IDX_HEADS
TOPK_MAX = 256
N_BUCKETS, MAX_DISTANCE = 32, 128
MEM_HEADS, MEM_HEAD, MEM_WIDTH = 4, 128, 512
N_EXPERTS, TOP_K, D_FF = 32, 4, 1024
SWIGLU_LIMIT, SWIGLU_ALPHA = 7.0, 1.702
LN_EPS = 1e-5
PAGE = 128

LANES = 128
CHUNK = 64
QBLK = 128
KCH = 512
NEG = -1e30
INT_MIN = -(2 ** 31)
INT_MAX = 2 ** 31 - 1
VMEM_LIMIT = 56 * 1024 * 1024


def _cp(n_axes, vmem=VMEM_LIMIT):
    return pltpu.CompilerParams(dimension_semantics=("arbitrary",) * n_axes,
                                vmem_limit_bytes=vmem)


def _dot(a, b):
    return jnp.dot(a.astype(BF16), b.astype(BF16), preferred_element_type=F32)


def _dot_nt(a, b):
    return lax.dot_general(a.astype(BF16), b.astype(BF16), (((1,), (1,)), ((), ())),
                           preferred_element_type=F32)


def _dot_tn(a, b):
    return lax.dot_general(a.astype(BF16), b.astype(BF16), (((0,), (0,)), ((), ())),
                           preferred_element_type=F32)


def _split2(a):
    hi = a.astype(BF16)
    lo = (a - hi.astype(F32)).astype(BF16)
    return hi, lo


def _split3(a):
    hi = a.astype(BF16)
    r1 = a - hi.astype(F32)
    mid = r1.astype(BF16)
    lo = (r1 - mid.astype(F32)).astype(BF16)
    return hi, mid, lo


def _dot_x(a, b_exact):
    hi, mid, lo = _split3(a)
    d = lambda x: jnp.dot(x, b_exact, preferred_element_type=F32)
    return d(hi) + d(mid) + d(lo)


def _xdot(b_exact, a):
    hi, mid, lo = _split3(a)
    d = lambda x: jnp.dot(b_exact, x, preferred_element_type=F32)
    return d(hi) + d(mid) + d(lo)


def _dot_hp(a, b):
    ah, al = _split2(a)
    bh, bl = _split2(b)
    d = lambda x, y: jnp.dot(x, y, preferred_element_type=F32)
    return d(ah, bh) + d(ah, bl) + d(al, bh)


def _layer_norm(z, g, b):
    mu = jnp.mean(z, axis=-1, keepdims=True)
    zc = z - mu
    var = jnp.mean(zc * zc, axis=-1, keepdims=True)
    return zc * lax.rsqrt(var + LN_EPS) * g + b


def _head_mask(hh):
    lane = lax.broadcasted_iota(I32, (1, LANES), 1)
    return (lane >= hh * 64) & (lane < (hh + 1) * 64)


def _proj_kernel(x_ref, w_ref, *o_refs, cols):
    acc = jnp.dot(x_ref[...].astype(BF16), w_ref[...], preferred_element_type=F32)
    for o_ref, (c0, wd, scale) in zip(o_refs, cols):
        piece = acc[:, c0:c0 + wd]
        if scale != 1.0:
            piece = piece * scale
        o_ref[...] = piece.astype(o_ref.dtype)


def _proj(x, w_bf, cols, dtypes, tm):
    n, kdim = x.shape
    m = w_bf.shape[1]
    tm = min(tm, n)
    grid = (n // tm,)
    return pl.pallas_call(
        functools.partial(_proj_kernel, cols=tuple(cols)),
        grid=grid,
        in_specs=[pl.BlockSpec((tm, kdim), lambda i: (i, 0)),
                  pl.BlockSpec((kdim, m), lambda i: (0, 0))],
        out_specs=[pl.BlockSpec((tm, wd), lambda i: (i, 0)) for (_, wd, _) in cols],
        out_shape=[jax.ShapeDtypeStruct((n, wd), dt) for (_, wd, _), dt in zip(cols, dtypes)],
        compiler_params=_cp(1),
        name="proj",
    )(x, w_bf)


def _rw_prep(p, p_prev, mu, w0, wB2, a0, aB2, gB, kkp, ka, segm):
    xs = p + (p_prev - p) * mu
    r = xs[:, 0:512]
    k = xs[:, 512:1024]
    v = xs[:, 1024:1536]
    lwla = xs[:, 1536:1664]
    lg = xs[:, 1664:1792]
    w = -jax.nn.softplus(-(w0 + _dot(jnp.tanh(lwla), wB2))) - 0.5
    ld = -jnp.exp(w)
    a = jax.nn.sigmoid(a0 + _dot(lwla, aB2))
    g = _dot(jax.nn.sigmoid(lg), gB)
    kkr = k * kkp
    ss = _dot_x(kkr * kkr, segm)
    kk = kkr / jnp.maximum(jnp.sqrt(ss), 1e-12)
    k2 = k * (1.0 + (a - 1.0) * ka)
    b = kk * a
    return r, k2, v, kk, b, ld, g


def _rw_chunk_kernel(p_ref, pb_ref, p0_ref, mu_ref, w0_ref, wB_ref, a0_ref, aB_ref, gB_ref,
                     kk_ref, ka_ref, rk_ref, segm_ref, ltri_ref,
                     y1_ref, y2_ref, bonus_ref, g_ref, ac_ref, cc_ref,
                     r_s, k_s, v_s, kk_s, b_s, ld_s, *, tm):
    i = pl.program_id(1)
    p = p_ref[...]
    prev_last = jnp.where(i == 0, p0_ref[...], pb_ref[7:8, :])
    rolled = pltpu.roll(p, 1, axis=0)
    row = lax.broadcasted_iota(I32, p.shape, 0)
    p_prev = jnp.where(row == 0, prev_last, rolled)
    segm = segm_ref[...]
    r, k2, v, kk, b, ld, g = _rw_prep(p, p_prev, mu_ref[...], w0_ref[...], wB_ref[...], a0_ref[...],
                                      aB_ref[...], gB_ref[...], kk_ref[...], ka_ref[...], segm)
    g_ref[...] = g
    bonus_ref[...] = _dot_x(r * k2 * rk_ref[...], segm) * v
    r_s[...] = r
    k_s[...] = k2
    v_s[...] = v
    kk_s[...] = kk
    b_s[...] = b
    ld_s[...] = ld

    ltri = ltri_ref[...]
    ri = lax.broadcasted_iota(I32, (CHUNK, CHUNK), 0)
    ci = lax.broadcasted_iota(I32, (CHUNK, CHUNK), 1)
    strict = ri > ci
    incl = ri >= ci
    eye = (lax.broadcasted_iota(I32, (LANES, LANES), 0) == lax.broadcasted_iota(I32, (LANES, LANES), 1))

    def chunk(c, carry):
        rows = pl.ds(pl.multiple_of(c * CHUNK, CHUNK), CHUNK)
        ldc = ld_s[rows, :]
        cum = _xdot(ltri, ldc)
        cum_c = cum[CHUNK - 1:CHUNK, :]
        pin = jnp.exp(cum)
        pex = jnp.exp(cum - ldc)
        pinv = jnp.exp(-cum)
        pend = jnp.exp(cum_c - cum)
        pc = jnp.exp(cum_c)
        kkc, bc, kc, rc, vc = kk_s[rows, :], b_s[rows, :], k_s[rows, :], r_s[rows, :], v_s[rows, :]
        at = -kkc * pex
        rt = rc * pin
        bt = bc * pinv
        kt = kc * pinv
        bend = bc * pend
        kend = kc * pend
        for pr in range(RW_HEADS // 2):
            ln = slice(pr * LANES, (pr + 1) * LANES)
            atp, rtp, btp, ktp, vp = at[:, ln], rt[:, ln], bt[:, ln], kt[:, ln], vc[:, ln]
            ar = jnp.concatenate([atp, rtp], axis=0)
            bkend = jnp.concatenate([bend[:, ln], kend[:, ln]], axis=0)
            zv = jnp.concatenate([jnp.zeros_like(vp), vp], axis=1)
            acp = jnp.where(eye, pc[:, ln], 0.0)
            ccp = jnp.zeros((LANES, LANES), F32)
            y1p = jnp.zeros((CHUNK, LANES), F32)
            y2p = jnp.zeros((CHUNK, LANES), F32)
            for hh in range(2):
                hm = _head_mask(hh)
                arm = jnp.where(hm, ar, 0.0)
                gb = _dot_nt(arm, btp)
                gk = _dot_nt(arm, ktp)
                nm = jnp.where(strict, gb[:CHUNK], 0.0)
                lrb = jnp.where(incl, gb[CHUNK:], 0.0)
                aak = jnp.where(strict, gk[:CHUNK], 0.0)
                lrk = jnp.where(incl, gk[CHUNK:], 0.0)
                x = jnp.concatenate([atp, _dot(aak, vp)], axis=1)
                n2 = _dot(nm, nm)
                n4 = _dot(n2, n2)
                n8 = _dot(n4, n4)
                n16 = _dot(n8, n8)
                n32 = _dot(n16, n16)
                for q in (n32, n16, n8, n4, n2, nm):
                    x = x + _dot(q, x)
                lhs = jnp.where(hm, bkend, 0.0)
                rhs = jnp.concatenate([x, zv], axis=0)
                res = _dot_tn(lhs, rhs)
                acp = acp + jnp.where(hm, res[:, :LANES], 0.0)
                ccp = ccp + jnp.where(hm, res[:, LANES:], 0.0)
                lx = _dot(lrb, x)
                y1p = y1p + jnp.where(hm, rtp + lx[:, :LANES], 0.0)
                y2p = y2p + jnp.where(hm, lx[:, LANES:] + _dot(lrk, vp), 0.0)
            y1_ref[rows, ln] = y1p
            y2_ref[rows, ln] = y2p
            ac_ref[c, :, ln] = acp
            cc_ref[c, :, ln] = ccp
        return carry

    lax.fori_loop(0, tm // CHUNK, chunk, 0)


def _rw_chunks(p_rw, prev0, lw, batch, seq, tm):
    n = batch * seq
    nt = seq // tm
    ncs = tm // CHUNK
    full = lambda shape: pl.BlockSpec(shape, lambda b, i: (0,) * len(shape))
    row_blk = pl.BlockSpec((tm, RW_WIDTH), lambda b, i: (b * nt + i, 0))
    mat_blk = pl.BlockSpec((ncs, LANES, RW_WIDTH), lambda b, i: (b * nt + i, 0, 0))
    return pl.pallas_call(
        functools.partial(_rw_chunk_kernel, tm=tm),
        grid=(batch, nt),
        in_specs=[pl.BlockSpec((tm, RW_IN), lambda b, i: (b * nt + i, 0)),
                  pl.BlockSpec((8, RW_IN), lambda b, i: (jnp.maximum((b * nt + i) * (tm // 8) - 1, 0), 0)),
                  pl.BlockSpec((None, 1, RW_IN), lambda b, i: (b, 0, 0)),
                  full((1, RW_IN)), full((1, RW_WIDTH)), full((LANES, RW_WIDTH)), full((1, RW_WIDTH)),
                  full((LANES, RW_WIDTH)), full((LANES, RW_WIDTH)), full((1, RW_WIDTH)), full((1, RW_WIDTH)),
                  full((1, RW_WIDTH)), full((RW_WIDTH, RW_WIDTH)), full((CHUNK, CHUNK))],
        out_specs=[row_blk, row_blk, row_blk, row_blk, mat_blk, mat_blk],
        out_shape=[jax.ShapeDtypeStruct((n, RW_WIDTH), F32)] * 4
                  + [jax.ShapeDtypeStruct((n // CHUNK, LANES, RW_WIDTH), F32)] * 2,
        scratch_shapes=[pltpu.VMEM((tm, RW_WIDTH), F32)] * 6,
        compiler_params=_cp(2),
        name="rwkv_chunks",
    )(p_rw, p_rw, prev0, lw["mu"], lw["w0"], lw["wB2"], lw["a0"], lw["aB2"], lw["gB"],
      lw["kk"], lw["ka"], lw["rk"], lw["segm"], lw["ltri"])


def _rw_scan_kernel(y1_ref, y2_ref, bonus_ref, g_ref, ac_ref, cc_ref, h0_ref, gng_ref, gnb_ref, segm_ref,
                    y_ref, hfin_ref, h_s, *, ncs):
    i = pl.program_id(1)

    @pl.when(i == 0)
    def _():
        h_s[...] = h0_ref[...]

    segm = segm_ref[...]
    for c in range(ncs):
        rows = slice(c * CHUNK, (c + 1) * CHUNK)
        parts = []
        for pr in range(RW_HEADS // 2):
            ln = slice(pr * LANES, (pr + 1) * LANES)
            hp = h_s[:, ln]
            parts.append(_dot_hp(y1_ref[rows, ln], hp) + y2_ref[rows, ln])
            h_s[:, ln] = _dot_hp(ac_ref[c, :, ln], hp) + cc_ref[c, :, ln]
        y = jnp.concatenate(parts, axis=1)
        mu = _dot_x(y, segm) * (1.0 / RW_HEAD)
        yc = y - mu
        var = _dot_x(yc * yc, segm) * (1.0 / RW_HEAD)
        yn = yc * lax.rsqrt(var + RW_GN_EPS) * gng_ref[...] + gnb_ref[...]
        y_ref[rows, :] = ((yn + bonus_ref[rows, :]) * g_ref[rows, :]).astype(y_ref.dtype)
    hfin_ref[...] = h_s[...]


def _rw_scan(y1, y2, bonus, g, ac, cc, h0, lw, batch, seq, tm):
    n = batch * seq
    nt = seq // tm
    ncs = tm // CHUNK
    full = lambda shape: pl.BlockSpec(shape, lambda b, i: (0,) * len(shape))
    row_blk = pl.BlockSpec((tm, RW_WIDTH), lambda b, i: (b * nt + i, 0))
    mat_blk = pl.BlockSpec((ncs, LANES, RW_WIDTH), lambda b, i: (b * nt + i, 0, 0))
    st_blk = pl.BlockSpec((None, LANES, RW_WIDTH), lambda b, i: (b, 0, 0))
    return pl.pallas_call(
        functools.partial(_rw_scan_kernel, ncs=ncs),
        grid=(batch, nt),
        in_specs=[row_blk, row_blk, row_blk, row_blk, mat_blk, mat_blk, st_blk,
                  full((1, RW_WIDTH)), full((1, RW_WIDTH)), full((RW_WIDTH, RW_WIDTH))],
        out_specs=[row_blk, st_blk],
        out_shape=[jax.ShapeDtypeStruct((n, RW_WIDTH), BF16),
                   jax.ShapeDtypeStruct((batch, LANES, RW_WIDTH), F32)],
        scratch_shapes=[pltpu.VMEM((LANES, RW_WIDTH), F32)],
        compiler_params=_cp(2),
        name="rwkv_scan",
    )(y1, y2, bonus, g, ac, cc, h0, lw["gn_g"], lw["gn_b"], lw["segm"])


def _rw_step_prep_kernel(p_ref, pp_ref, mu_ref, w0_ref, wB_ref, a0_ref, aB_ref, gB_ref,
                         kk_ref, ka_ref, segm_ref, r_o, k_o, v_o, kk_o, b_o, w_o, g_o):
    r, k2, v, kk, b, ld, g = _rw_prep(p_ref[...], pp_ref[...], mu_ref[...], w0_ref[...], wB_ref[...],
                                      a0_ref[...], aB_ref[...], gB_ref[...], kk_ref[...], ka_ref[...],
                                      segm_ref[...])
    r_o[...] = r
    k_o[...] = k2
    v_o[...] = v
    kk_o[...] = kk
    b_o[...] = b
    w_o[...] = jnp.exp(ld)
    g_o[...] = g


def _rw_step_prep(p, p_prev, lw):
    n = p.shape[0]
    return pl.pallas_call(
        _rw_step_prep_kernel,
        out_shape=[jax.ShapeDtypeStruct((n, RW_WIDTH), F32)] * 7,
        compiler_params=pltpu.CompilerParams(vmem_limit_bytes=VMEM_LIMIT),
        name="rwkv_step_prep",
    )(p, p_prev, lw["mu"], lw["w0"], lw["wB2"], lw["a0"], lw["aB2"], lw["gB"], lw["kk"], lw["ka"], lw["segm"])


def _rw_step_kernel(s_ref, r_ref, w_ref, k_ref, kk_ref, b_ref, v_ref, g_ref, rk_ref, gng_ref, gnb_ref,
                    s_o, y_o):
    s = s_ref[...]
    r, w, k, kk, b = r_ref[...], w_ref[...], k_ref[...], kk_ref[...], b_ref[...]
    v = v_ref[...]
    rb = lambda a: a.astype(BF16).astype(F32)
    sa = jnp.sum(rb(s) * rb(-kk), axis=-1, keepdims=True)
    sn = s * w + sa * b + v * k
    s_o[...] = sn
    y = jnp.sum(rb(sn) * rb(r), axis=-1, keepdims=True)
    mu = jnp.mean(y, axis=1, keepdims=True)
    yc = y - mu
    var = jnp.mean(yc * yc, axis=1, keepdims=True)
    yn = yc * lax.rsqrt(var + RW_GN_EPS) * gng_ref[...] + gnb_ref[...]
    bonus = jnp.sum(r * k * rk_ref[...], axis=-1, keepdims=True) * v
    y_o[...] = (yn + bonus) * g_ref[...]


def _rw_step(s0, r, w, k, kk, b, v, g, lw):
    db = s0.shape[0]
    hk = lambda a: a.reshape(db, RW_HEADS, 1, RW_HEAD)
    hv = lambda a: a.reshape(db, RW_HEADS, RW_HEAD, 1)
    rowk = pl.BlockSpec((None, RW_HEADS, 1, RW_HEAD), lambda i: (i, 0, 0, 0))
    colv = pl.BlockSpec((None, RW_HEADS, RW_HEAD, 1), lambda i: (i, 0, 0, 0))
    st = pl.BlockSpec((None, RW_HEADS, RW_HEAD, RW_HEAD), lambda i: (i, 0, 0, 0))
    prk = pl.BlockSpec((RW_HEADS, 1, RW_HEAD), lambda i: (0, 0, 0))
    pcv = pl.BlockSpec((RW_HEADS, RW_HEAD, 1), lambda i: (0, 0, 0))
    s_new, y = pl.pallas_call(
        _rw_step_kernel,
        grid=(db,),
        in_specs=[st, rowk, rowk, rowk, rowk, rowk, colv, colv, prk, pcv, pcv],
        out_specs=[st, colv],
        out_shape=[jax.ShapeDtypeStruct(s0.shape, F32),
                   jax.ShapeDtypeStruct((db, RW_HEADS, RW_HEAD, 1), F32)],
        compiler_params=_cp(1),
        name="rwkv_step",
    )(s0, hk(r), hk(w), hk(k), hk(kk), hk(b), hv(v), hv(g),
      lw["rk"].reshape(RW_HEADS, 1, RW_HEAD), lw["gn_g"].reshape(RW_HEADS, RW_HEAD, 1),
      lw["gn_b"].reshape(RW_HEADS, RW_HEAD, 1))
    return s_new, y.reshape(db, RW_WIDTH)


def _sortable(x):
    bits = pltpu.bitcast(x, I32)
    return jnp.where(bits < 0, bits ^ jnp.int32(INT_MAX), bits)


def _dsa_prompt_kernel(iq_ref, iw_ref, q_ref, ik_ref, k_ref, v_ref, bias_ref, o_ref,
                       ksc, mbsc, wbsc, iqm, qmsc, tsc, jsc, *, topk):
    j = pl.program_id(1)
    nk = j + 1
    ones_bf = jnp.ones((LANES, LANES), BF16)
    row = lax.broadcasted_iota(I32, (QBLK, LANES), 0)
    col = lax.broadcasted_iota(I32, (QBLK, LANES), 1)
    qpos = j * QBLK + row
    wscale = IDX_HEADS ** -0.5 * IDX_HEAD ** -0.5

    iw = iw_ref[...]
    for h in range(IDX_HEADS):
        wbsc[h] = jnp.broadcast_to(iw[:, h:h + 1] * wscale, (QBLK, LANES))
        iqp = iq_ref[:, (h // 2) * LANES:(h // 2 + 1) * LANES]
        iqm[h] = jnp.where(_head_mask(h % 2), iqp, jnp.zeros_like(iqp))

    def score_chunk(c, carry):
        c0 = pl.multiple_of(c * KCH, KCH)
        ikc = ik_ref[pl.ds(c0, KCH), :]
        acc = jnp.zeros((QBLK, KCH), F32)
        for h in range(IDX_HEADS):
            s = _dot_nt(iqm[h], ikc)
            wb = wbsc[h]
            acc = acc + jnp.maximum(s, 0.0) * jnp.concatenate([wb] * (KCH // LANES), axis=1)
        key = _sortable(acc)
        for t in range(KCH // LANES):
            kb = c * (KCH // LANES) + t
            valid = (kb * LANES + col) <= qpos
            ksc[kb] = jnp.where(valid, key[:, t * LANES:(t + 1) * LANES], jnp.int32(INT_MIN))
        return carry

    lax.fori_loop(0, j // (KCH // LANES) + 1, score_chunk, 0)

    def lane_total(cnt):
        return jnp.dot(cnt.astype(BF16), ones_bf, preferred_element_type=F32)

    def count(pred):
        def body(kb, cnt):
            return cnt + pred(kb, ksc[kb]).astype(I32)
        return lane_total(lax.fori_loop(0, nk, body, jnp.zeros((QBLK, LANES), I32)))

    ktop = jnp.float32(topk)

    def bisect(_, lohi):
        lo, hi = lohi
        mid = lo + lax.shift_right_logical(hi - lo, 1)
        ok = count(lambda kb, key: key >= mid) >= ktop
        return jnp.where(ok, mid, lo), jnp.where(ok, hi, mid)

    lo, _ = lax.fori_loop(0, 32, bisect, (jnp.full((QBLK, LANES), INT_MIN, I32),
                                           jnp.full((QBLK, LANES), INT_MAX, I32)))
    thr = lo
    tsc[...] = thr
    jsc[...] = jnp.full((QBLK, LANES), INT_MAX, I32)
    c_ge = count(lambda kb, key: key >= thr)

    @pl.when(jnp.max(c_ge) > ktop)
    def _ties():
        need = ktop - count(lambda kb, key: key > thr)

        def bis_j(_, lohi):
            lo_j, hi_j = lohi
            mid = lax.shift_right_arithmetic(lo_j + hi_j, 1)
            ok = count(lambda kb, key: (key == thr) & ((kb * LANES + col) <= mid)) >= need
            return jnp.where(ok, lo_j, mid), jnp.where(ok, mid, hi_j)

        _, hi_j = lax.fori_loop(0, 15, bis_j, (jnp.full((QBLK, LANES), -1, I32),
                                               jnp.full((QBLK, LANES), 1, I32) * (nk * LANES - 1)))
        jsc[...] = hi_j

    thr = tsc[...]
    jlim = jsc[...]

    def mask_block(kb, carry):
        key = ksc[kb]
        lpos = kb * LANES + col
        sel = (key > thr) | ((key == thr) & (lpos <= jlim))
        mbsc[kb] = jnp.where(sel & (lpos <= qpos), 0.0, NEG)
        return carry

    lax.fori_loop(0, nk, mask_block, 0)

    nfull = jnp.maximum(j - 1, 0) // (KCH // LANES)
    pair = lambda h: slice((h // 2) * LANES, (h // 2 + 1) * LANES)
    for h in range(AT_HEADS):
        qp = q_ref[:, pair(h)]
        qmsc[h] = jnp.where(_head_mask(h % 2), qp, jnp.zeros_like(qp))

    def update(carry, s, vc):
        m, l, acc = carry
        mn = jnp.maximum(m, jnp.max(s, axis=1, keepdims=True))
        alpha = jnp.exp(m - mn)
        p = jnp.exp(s - mn)
        l = l * alpha + jnp.sum(p, axis=1, keepdims=True)
        acc = acc * alpha + jnp.dot(p.astype(BF16), vc, preferred_element_type=F32)
        return mn, l, acc

    def far(c, carry):
        c0 = pl.multiple_of(c * KCH, KCH)
        mb = jnp.concatenate([mbsc[c * (KCH // LANES) + t] for t in range(KCH // LANES)], axis=1)
        new = []
        for h in range(AT_HEADS):
            kc = k_ref[pl.ds(c0, KCH), pair(h)]
            vc = v_ref[pl.ds(c0, KCH), pair(h)]
            new.append(update(carry[h], _dot_nt(qmsc[h], kc) + mb, vc))
        return tuple(new)

    def near(kb, carry):
        k0 = pl.multiple_of(kb * LANES, LANES)
        wsel = jnp.clip(kb - j + 2, 0, 2)
        mb = mbsc[kb]
        new = []
        for h in range(AT_HEADS):
            kc = k_ref[pl.ds(k0, LANES), pair(h)]
            vc = v_ref[pl.ds(k0, LANES), pair(h)]
            new.append(update(carry[h], _dot_nt(qmsc[h], kc) + mb + bias_ref[wsel, h], vc))
        return tuple(new)

    init = tuple((jnp.full((QBLK, 1), NEG, F32), jnp.zeros((QBLK, 1), F32), jnp.zeros((QBLK, LANES), F32))
                 for _ in range(AT_HEADS))
    carry = lax.fori_loop(0, nfull, far, init)
    carry = lax.fori_loop(nfull * (KCH // LANES), nk, near, carry)
    outs = []
    for pr in range(AT_HEADS // 2):
        (_, l0, a0), (_, l1, a1) = carry[2 * pr], carry[2 * pr + 1]
        outs.append(jnp.where(_head_mask(0), a0 / l0, a1 / l1))
    o_ref[...] = jnp.concatenate(outs, axis=1).astype(o_ref.dtype)


def _dsa_prompt(iq_bf, iw, q_bf, ik2_bf, k_bf, v_bf, bias3, batch, seq):
    n = batch * seq
    nqb = seq // QBLK
    topk = min(TOPK_MAX, seq // 4)
    assert seq % KCH == 0
    qblk = lambda wd: pl.BlockSpec((QBLK, wd), lambda b, j: (b * nqb + j, 0))
    whole = lambda wd: pl.BlockSpec((seq, wd), lambda b, j: (b, 0), pipeline_mode=pl.Buffered(1))
    return pl.pallas_call(
        functools.partial(_dsa_prompt_kernel, topk=topk),
        grid=(batch, nqb),
        in_specs=[qblk(AT_WIDTH), qblk(IDX_HEADS), qblk(AT_WIDTH), whole(LANES), whole(AT_WIDTH), whole(AT_WIDTH),
                  pl.BlockSpec((3, AT_HEADS, QBLK, LANES), lambda b, j: (0, 0, 0, 0))],
        out_specs=qblk(AT_WIDTH),
        out_shape=jax.ShapeDtypeStruct((n, AT_WIDTH), BF16),
        scratch_shapes=[pltpu.VMEM((seq // LANES, QBLK, LANES), I32),
                        pltpu.VMEM((seq // LANES, QBLK, LANES), F32),
                        pltpu.VMEM((IDX_HEADS, QBLK, LANES), F32),
                        pltpu.VMEM((IDX_HEADS, QBLK, LANES), BF16),
                        pltpu.VMEM((AT_HEADS, QBLK, LANES), BF16),
                        pltpu.VMEM((QBLK, LANES), I32),
                        pltpu.VMEM((QBLK, LANES), I32)],
        compiler_params=_cp(2),
        name="dsa_prompt",
    )(iq_bf, iw, q_bf, ik2_bf, k_bf, v_bf, bias3)


def _dsa_sel_kernel(pt_ref, iq_ref, w_ref, ikn_ref, cik_hbm, mb_ref, ikbuf, sem, *, layer, past, topk, lp2):
    b = pl.program_id(0)
    n_pages = past // PAGE
    lpad = past + PAGE

    def page_copy(pg):
        return pltpu.make_async_copy(cik_hbm.at[layer, pt_ref[b, pg]],
                                     ikbuf.at[pl.ds(pl.multiple_of(pg * PAGE, PAGE), PAGE), :], sem)

    def issue(pg, c):
        page_copy(pg).start()
        return c

    lax.fori_loop(0, n_pages, issue, 0)
    ikbuf[pl.ds(past, PAGE), :] = jnp.zeros((PAGE, IDX_HEAD), F32)
    ikbuf[pl.ds(past, 1), :] = ikn_ref[...]

    def wait(pg, c):
        page_copy(pg).wait()
        return c

    lax.fori_loop(0, n_pages, wait, 0)

    iq = iq_ref[...]
    w = (w_ref[...] * (IDX_HEADS ** -0.5)).astype(BF16).astype(F32)
    pieces = []
    step = 2048
    for c0 in list(range(0, past, step)) + [past]:
        wd = min(step, past - c0) if c0 < past else PAGE
        s = _dot_nt(iq, ikbuf[pl.ds(c0, wd), :])
        rl = (jnp.maximum(s, 0.0) * (IDX_HEAD ** -0.5)).astype(BF16).astype(F32)
        sc = jnp.sum(rl * w, axis=0, keepdims=True) + 0.0
        pieces.append(sc)
    score = jnp.concatenate(pieces, axis=1)
    pos = lax.broadcasted_iota(I32, (1, lpad), 1)
    key = jnp.where(pos <= past, _sortable(score), jnp.int32(INT_MIN))
    ktop = jnp.float32(topk)

    def count(pred):
        return jnp.sum(pred.astype(F32), axis=1, keepdims=True)

    def bisect(_, lohi):
        lo, hi = lohi
        mid = lo + lax.shift_right_logical(hi - lo, 1)
        ok = count(key >= mid) >= ktop
        return jnp.where(ok, mid, lo), jnp.where(ok, hi, mid)

    thr, _ = lax.fori_loop(0, 32, bisect, (jnp.full((1, 1), INT_MIN, I32), jnp.full((1, 1), INT_MAX, I32)))
    need = ktop - count(key > thr)

    def bis_j(_, lohi):
        lo_j, hi_j = lohi
        mid = lax.shift_right_arithmetic(lo_j + hi_j, 1)
        ok = count((key == thr) & (pos <= mid)) >= need
        return jnp.where(ok, lo_j, mid), jnp.where(ok, mid, hi_j)

    _, jlim = lax.fori_loop(0, 16, bis_j, (jnp.full((1, 1), -1, I32), jnp.full((1, 1), lpad - 1, I32)))
    sel = (key > thr) | ((key == thr) & (pos <= jlim))
    mb_ref[:, :lpad] = jnp.where(sel & (pos <= past), 0.0, NEG)
    if lp2 > lpad:
        mb_ref[:, lpad:] = jnp.full((1, lp2 - lpad), NEG, F32)


def _sel_rows(past):
    return -(-(past // PAGE + 1) // LANES) * LANES


def _dsa_sel(page_table, iq_bf, iw, ik_new, cache_idx_k, layer, past):
    db = iq_bf.shape[0]
    lpad = past + PAGE
    lp2 = _sel_rows(past) * LANES
    topk = min(TOPK_MAX, (past + 1) // 4)
    gs = pltpu.PrefetchScalarGridSpec(
        num_scalar_prefetch=1,
        grid=(db,),
        in_specs=[pl.BlockSpec((None, IDX_HEADS, IDX_HEAD), lambda b, pt: (b, 0, 0)),
                  pl.BlockSpec((None, IDX_HEADS, 1), lambda b, pt: (b, 0, 0)),
                  pl.BlockSpec((None, 1, IDX_HEAD), lambda b, pt: (b, 0, 0)),
                  pl.BlockSpec(memory_space=pl.ANY)],
        out_specs=pl.BlockSpec((None, 1, lp2), lambda b, pt: (b, 0, 0)),
        scratch_shapes=[pltpu.VMEM((lpad, IDX_HEAD), F32), pltpu.SemaphoreType.DMA(())],
    )
    return pl.pallas_call(
        functools.partial(_dsa_sel_kernel, layer=layer, past=past, topk=topk, lp2=lp2),
        grid_spec=gs,
        out_shape=jax.ShapeDtypeStruct((db, 1, lp2), F32),
        compiler_params=_cp(1),
        name="dsa_sample_select",
    )(page_table, iq_bf.reshape(db, IDX_HEADS, IDX_HEAD), iw.reshape(db, IDX_HEADS, 1),
      ik_new.reshape(db, 1, IDX_HEAD), cache_idx_k)


def _dsa_compact_kernel(mb_ref, idx_ref, rank_s, *, topk, used_rows):
    nrows = mb_ref.shape[0]
    sel = mb_ref[...] == 0.0
    selb = jnp.where(sel, 1.0, 0.0).astype(BF16)
    li = lax.broadcasted_iota(I32, (LANES, LANES), 0)
    lj = lax.broadcasted_iota(I32, (LANES, LANES), 1)
    within = jnp.dot(selb, jnp.where(li < lj, 1.0, 0.0).astype(BF16), preferred_element_type=F32)
    rtot = jnp.dot(selb, jnp.ones((LANES, LANES), BF16), preferred_element_type=F32)
    ri = lax.broadcasted_iota(I32, (nrows, nrows), 0)
    rj = lax.broadcasted_iota(I32, (nrows, nrows), 1)
    roff = jnp.dot(jnp.where(rj < ri, 1.0, 0.0).astype(BF16), rtot.astype(BF16), preferred_element_type=F32)
    rank_s[...] = jnp.where(sel, roff + within, -1.0)
    r_io = lax.broadcasted_iota(I32, (topk, LANES), 0).astype(F32)
    lane = lax.broadcasted_iota(I32, (topk, LANES), 1)

    def body(i, acc):
        hit = rank_s[pl.ds(i, 1), :] == r_io
        return acc + jnp.where(hit, (i * LANES + lane).astype(F32), 0.0)

    acc = lax.fori_loop(0, used_rows, body, jnp.zeros((topk, LANES), F32))
    idx_ref[...] = jnp.sum(acc, axis=1, keepdims=True).astype(I32)


def _dsa_compact(mb2, topk, used_rows):
    db, nrows, _ = mb2.shape
    return pl.pallas_call(
        functools.partial(_dsa_compact_kernel, topk=topk, used_rows=used_rows),
        grid=(db,),
        in_specs=[pl.BlockSpec((None, nrows, LANES), lambda b: (b, 0, 0))],
        out_specs=pl.BlockSpec((None, topk, 1), lambda b: (b, 0, 0)),
        out_shape=jax.ShapeDtypeStruct((db, topk, 1), I32),
        scratch_shapes=[pltpu.VMEM((nrows, LANES), F32)],
        compiler_params=_cp(1),
        name="dsa_sample_compact",
    )(mb2)


def _dsa_gather_kernel(pt_ref, idx_ref, q_ref, kn_ref, vn_ref, btab_ref, ck_hbm, cv_hbm, o_ref,
                       kbuf, vbuf, bias_s, sem_k, sem_v, *, layer, past, nsel):
    b = pl.program_id(0)
    page_shift = PAGE.bit_length() - 1

    def k_copy(r, phys, off):
        return pltpu.make_async_copy(ck_hbm.at[layer, phys, off], kbuf.at[r], sem_k)

    def v_copy(r, phys, off):
        return pltpu.make_async_copy(cv_hbm.at[layer, phys, off], vbuf.at[r], sem_v)

    def issue(r, c):
        i = idx_ref[b, r]
        ii = jnp.minimum(i, past - 1)
        phys = pt_ref[b, lax.shift_right_logical(ii, page_shift)]
        off = jnp.bitwise_and(ii, PAGE - 1)

        @pl.when(i < past)
        def _():
            k_copy(r, phys, off).start()
            v_copy(r, phys, off).start()

        bias_s[r] = btab_ref[jnp.clip(past - i, 0, PAGE)]
        return c

    lax.fori_loop(0, nsel, issue, 0)

    def wait(r, c):
        i = idx_ref[b, r]

        @pl.when(i < past)
        def _():
            k_copy(r, 0, 0).wait()
            v_copy(r, 0, 0).wait()

        @pl.when(i >= past)
        def _():
            kbuf[r] = kn_ref[...]
            vbuf[r] = vn_ref[...]

        return c

    lax.fori_loop(0, nsel, wait, 0)
    kg = kbuf[0:nsel]
    vg = vbuf[0:nsel]
    lg = jnp.sum(kg * q_ref[...][None], axis=-1, keepdims=True) * (AT_HEAD ** -0.5) + bias_s[0:nsel]
    e = jnp.exp(lg - jnp.max(lg, axis=0, keepdims=True))
    p = e / jnp.sum(e, axis=0, keepdims=True)
    o_ref[...] = jnp.sum(p * vg, axis=0).astype(o_ref.dtype)


def _dsa_gather(page_table, idx, q, k_new, v_new, btab, cache_k, cache_v, layer, past):
    db, topk = idx.shape
    nsel = min(topk, past + 1)
    hd = lambda a: a.reshape(db, AT_HEADS, AT_HEAD)
    row = pl.BlockSpec((None, AT_HEADS, AT_HEAD), lambda b, pt, ix: (b, 0, 0))
    gs = pltpu.PrefetchScalarGridSpec(
        num_scalar_prefetch=2,
        grid=(db,),
        in_specs=[row, row, row,
                  pl.BlockSpec((PAGE + 1, AT_HEADS, 1), lambda b, pt, ix: (0, 0, 0)),
                  pl.BlockSpec(memory_space=pl.ANY), pl.BlockSpec(memory_space=pl.ANY)],
        out_specs=row,
        scratch_shapes=[pltpu.VMEM((topk, AT_HEADS, AT_HEAD), F32), pltpu.VMEM((topk, AT_HEADS, AT_HEAD), F32),
                        pltpu.VMEM((topk, AT_HEADS, 1), F32),
                        pltpu.SemaphoreType.DMA(()), pltpu.SemaphoreType.DMA(())],
    )
    out = pl.pallas_call(
        functools.partial(_dsa_gather_kernel, layer=layer, past=past, nsel=nsel),
        grid_spec=gs,
        out_shape=jax.ShapeDtypeStruct((db, AT_HEADS, AT_HEAD), BF16),
        compiler_params=_cp(1),
        name="dsa_sample_gather_attend",
    )(page_table, idx, hd(q), hd(k_new), hd(v_new), btab, cache_k, cache_v)
    return out.reshape(db, AT_WIDTH)


def _mem_kernel(x_ref, w_ref, mk_ref, mv_ref, o_ref):
    mq = jnp.dot(x_ref[...].astype(BF16), w_ref[...], preferred_element_type=F32)
    outs = []
    for h in range(MEM_HEADS):
        ln = slice(h * MEM_HEAD, (h + 1) * MEM_HEAD)
        s = _dot_nt(mq[:, ln], mk_ref[:, ln]) * (MEM_HEAD ** -0.5)
        s = s - jnp.max(s, axis=1, keepdims=True)
        p = jnp.exp(s)
        p = p / jnp.sum(p, axis=1, keepdims=True)
        outs.append(_dot(p, mv_ref[:, ln]))
    o_ref[...] = jnp.concatenate(outs, axis=1).astype(o_ref.dtype)


def _mem_attend(x3, w_mq_bf, mk, mv, tm):
    bm, t, _ = x3.shape
    tm = min(tm, t)
    mlen = mk.shape[1]
    return pl.pallas_call(
        _mem_kernel,
        grid=(bm, t // tm),
        in_specs=[pl.BlockSpec((None, tm, D_MODEL), lambda b, i: (b, i, 0)),
                  pl.BlockSpec((D_MODEL, MEM_WIDTH), lambda b, i: (0, 0)),
                  pl.BlockSpec((None, mlen, MEM_WIDTH), lambda b, i: (b, 0, 0)),
                  pl.BlockSpec((None, mlen, MEM_WIDTH), lambda b, i: (b, 0, 0))],
        out_specs=pl.BlockSpec((None, tm, MEM_WIDTH), lambda b, i: (b, i, 0)),
        out_shape=jax.ShapeDtypeStruct((bm, t, MEM_WIDTH), BF16),
        compiler_params=_cp(2),
        name="mem_attend",
    )(x3, w_mq_bf, mk, mv)


def _merge_kernel(x_ref, yrw_ref, yat_ref, ymem_ref, wg_ref, wrw_ref, wat_ref, wmem_ref, wout_ref,
                  g1_ref, b1_ref, rw_ref, rb_ref, x1_ref, ti_ref, tg_ref, *, alpha):
    x = x_ref[...]
    gates = jax.nn.sigmoid(jnp.dot(x.astype(BF16), wg_ref[...], preferred_element_type=F32))
    d = D_MODEL
    merged = (gates[:, 0:d] * jnp.dot(yrw_ref[...], wrw_ref[...], preferred_element_type=F32)
              + gates[:, d:2 * d] * jnp.dot(yat_ref[...], wat_ref[...], preferred_element_type=F32)
              + gates[:, 2 * d:3 * d] * jnp.dot(ymem_ref[...], wmem_ref[...], preferred_element_type=F32))
    mix = jnp.dot(merged.astype(BF16), wout_ref[...], preferred_element_type=F32)
    x1 = _layer_norm(alpha * x + mix, g1_ref[...], b1_ref[...])
    x1_ref[...] = x1
    logits = _dot(x1, rw_ref[...]) + rb_ref[...]
    lane = lax.broadcasted_iota(I32, logits.shape, 1)
    idx_out = jnp.zeros(logits.shape, I32)
    val_out = jnp.full(logits.shape, NEG, F32)
    for r in range(TOP_K):
        m = jnp.max(logits, axis=1, keepdims=True)
        am = jnp.min(jnp.where(logits == m, lane, LANES), axis=1, keepdims=True)
        idx_out = jnp.where(lane == r, am, idx_out)
        val_out = jnp.where(lane == r, m, val_out)
        logits = jnp.where(lane == am, -jnp.inf, logits)
    e = jnp.exp(val_out - jnp.max(val_out, axis=1, keepdims=True))
    tg_ref[...] = e / jnp.sum(e, axis=1, keepdims=True)
    ti_ref[...] = idx_out


def _merge(x, y_rw, y_at, y_mem, lw, tm, alpha):
    n = x.shape[0]
    tm = min(tm, n)
    full = lambda a: pl.BlockSpec(a.shape, lambda i: (0,) * a.ndim)
    rowb = lambda wd: pl.BlockSpec((tm, wd), lambda i: (i, 0))
    ws = [lw["w_gate"], lw["w_br_rw"], lw["w_br_at"], lw["w_br_mem"], lw["w_out"],
          lw["ln1_g"], lw["ln1_b"], lw["router_w"], lw["router_b"]]
    return pl.pallas_call(
        functools.partial(_merge_kernel, alpha=alpha),
        grid=(n // tm,),
        in_specs=[rowb(D_MODEL), rowb(RW_WIDTH), rowb(AT_WIDTH), rowb(MEM_WIDTH)] + [full(a) for a in ws],
        out_specs=[rowb(D_MODEL), rowb(LANES), rowb(LANES)],
        out_shape=[jax.ShapeDtypeStruct((n, D_MODEL), F32), jax.ShapeDtypeStruct((n, LANES), I32),
                   jax.ShapeDtypeStruct((n, LANES), F32)],
        compiler_params=_cp(1),
        name="merge_ln_router",
    )(x, y_rw, y_at, y_mem, *ws)


def _ffn(xb, wg, bg, wu, bu, wd, bd):
    hg = jnp.minimum(jnp.dot(xb, wg.astype(BF16), preferred_element_type=F32) + bg, SWIGLU_LIMIT)
    hl = jnp.clip(jnp.dot(xb, wu.astype(BF16), preferred_element_type=F32) + bu, -SWIGLU_LIMIT, SWIGLU_LIMIT)
    h = hg * jax.nn.sigmoid(SWIGLU_ALPHA * hg) * (hl + 1.0)
    return jnp.dot(h.astype(BF16), wd.astype(BF16), preferred_element_type=F32) + bd


def _moe_expert_kernel(be_ref, nu_ref, tok_ref, x_hbm, gate_ref, wg_ref, bg_ref, wu_ref, bu_ref, wd_ref, bd_ref,
                       o_ref, xbuf, sem, *, blk):
    i = pl.program_id(0)

    def row_copy(r, tok):
        return pltpu.make_async_copy(x_hbm.at[pl.ds(tok, 1), :], xbuf.at[pl.ds(r, 1), :], sem)

    @pl.when(i < nu_ref[0])
    def _():
        def issue(r, c):
            row_copy(r, tok_ref[0, 0, r]).start()
            return c

        lax.fori_loop(0, blk, issue, 0, unroll=8)

        def wait(r, c):
            row_copy(r, 0).wait()
            return c

        lax.fori_loop(0, blk, wait, 0, unroll=8)
        out = _ffn(xbuf[...].astype(BF16), wg_ref[...], bg_ref[...], wu_ref[...], bu_ref[...],
                   wd_ref[...], bd_ref[...])
        o_ref[...] = out * gate_ref[...]

    @pl.when(i >= nu_ref[0])
    def _():
        o_ref[...] = jnp.zeros(o_ref.shape, F32)


def _moe_experts(blk_exp, n_used, buf_tok, x1, buf_gate, moe, layer, blk):
    n_blocks = blk_exp.shape[0]
    cap = n_blocks * blk
    wspec = pl.BlockSpec((None, None, D_MODEL, D_FF), lambda i, be, nu: (layer, be[i], 0, 0))
    wdspec = pl.BlockSpec((None, None, D_FF, D_MODEL), lambda i, be, nu: (layer, be[i], 0, 0))
    bspec = pl.BlockSpec((None, None, 1, D_FF), lambda i, be, nu: (layer, be[i], 0, 0))
    gs = pltpu.PrefetchScalarGridSpec(
        num_scalar_prefetch=2,
        grid=(n_blocks,),
        in_specs=[pl.BlockSpec((1, 1, blk), lambda i, be, nu: (i, 0, 0), memory_space=pltpu.SMEM),
                  pl.BlockSpec(memory_space=pl.ANY),
                  pl.BlockSpec((blk, 1), lambda i, be, nu: (i, 0)),
                  wspec, bspec, wspec, bspec, wdspec, bspec],
        out_specs=pl.BlockSpec((blk, D_MODEL), lambda i, be, nu: (i, 0)),
        scratch_shapes=[pltpu.VMEM((blk, D_MODEL), F32), pltpu.SemaphoreType.DMA(())],
    )
    e = N_EXPERTS
    depth = moe["w_gate"].shape[0]
    b4 = lambda a: a.reshape(depth, e, 1, a.shape[-1])
    return pl.pallas_call(
        functools.partial(_moe_expert_kernel, blk=blk),
        grid_spec=gs,
        out_shape=jax.ShapeDtypeStruct((cap, D_MODEL), F32),
        compiler_params=_cp(1),
        name="moe_experts",
    )(blk_exp, n_used, buf_tok.reshape(n_blocks, 1, blk), x1, buf_gate.reshape(cap, 1),
      moe["w_gate"], b4(moe["b_gate"]), moe["w_up"], b4(moe["b_up"]), moe["w_down"], b4(moe["b_down"]))


def _moe_combine_kernel(dest_ref, o_hbm, x1_ref, g2_ref, b2_ref, y_ref, buf, sem, *, tmc, alpha):
    def row_copy(jj, t, d):
        return pltpu.make_async_copy(o_hbm.at[pl.ds(d, 1), :], buf.at[jj, pl.ds(t, 1), :], sem)

    for jj in range(TOP_K):
        def issue(t, c):
            row_copy(jj, t, dest_ref[0, jj, t]).start()
            return c

        lax.fori_loop(0, tmc, issue, 0, unroll=8)
    for jj in range(TOP_K):
        def wait(t, c):
            row_copy(jj, t, 0).wait()
            return c

        lax.fori_loop(0, tmc, wait, 0, unroll=8)
    y = (buf[0] + buf[1]) + (buf[2] + buf[3])
    y_ref[...] = _layer_norm(alpha * x1_ref[...] + y, g2_ref[...], b2_ref[...])


def _moe_combine(dest, out_sorted, x1, ln2_g, ln2_b, tmc, alpha):
    n = x1.shape[0]
    nt = n // tmc
    dest3 = dest.reshape(nt, tmc, TOP_K).transpose(0, 2, 1)
    return pl.pallas_call(
        functools.partial(_moe_combine_kernel, tmc=tmc, alpha=alpha),
        grid=(nt,),
        in_specs=[pl.BlockSpec((1, TOP_K, tmc), lambda i: (i, 0, 0), memory_space=pltpu.SMEM),
                  pl.BlockSpec(memory_space=pl.ANY),
                  pl.BlockSpec((tmc, D_MODEL), lambda i: (i, 0)),
                  pl.BlockSpec((1, D_MODEL), lambda i: (0, 0)),
                  pl.BlockSpec((1, D_MODEL), lambda i: (0, 0))],
        out_specs=pl.BlockSpec((tmc, D_MODEL), lambda i: (i, 0)),
        out_shape=jax.ShapeDtypeStruct((n, D_MODEL), F32),
        scratch_shapes=[pltpu.VMEM((TOP_K, tmc, D_MODEL), F32), pltpu.SemaphoreType.DMA(())],
        compiler_params=_cp(1),
        name="moe_combine_ln",
    )(dest3, out_sorted, x1, ln2_g, ln2_b)


def _moe_prompt(x1, top_idx, top_gate, moe, layer, ln2_g, ln2_b, alpha, blk, tmc):
    n = x1.shape[0]
    n_asg = n * TOP_K
    e = N_EXPERTS
    n_blocks = n_asg // blk + e
    cap = n_blocks * blk
    expert = top_idx.reshape(n_asg)
    token = jnp.repeat(jnp.arange(n, dtype=I32), TOP_K)
    order = jnp.argsort(expert, stable=True)
    e_s = expert[order]
    counts = jnp.bincount(expert, length=e)
    starts = jnp.cumsum(counts) - counts
    padded = (counts + blk - 1) // blk * blk
    pends = jnp.cumsum(padded)
    pstarts = pends - padded
    dest_sorted = (pstarts[e_s] + jnp.arange(n_asg) - starts[e_s]).astype(I32)
    buf_tok = jnp.zeros((cap,), I32).at[dest_sorted].set(token[order])
    buf_gate = jnp.zeros((cap,), F32).at[dest_sorted].set(top_gate.reshape(n_asg)[order])
    blk_exp = jnp.minimum(jnp.searchsorted(pends, jnp.arange(n_blocks) * blk, side="right"), e - 1).astype(I32)
    n_used = (pends[-1] // blk).astype(I32).reshape(1)
    dest = jnp.zeros((n_asg,), I32).at[order].set(dest_sorted)
    out_sorted = _moe_experts(blk_exp, n_used, buf_tok, x1, buf_gate, moe, layer, blk)
    return _moe_combine(dest, out_sorted, x1, ln2_g, ln2_b, tmc, alpha)


def _moe_dense_kernel(x_ref, ti_ref, tg_ref, wg_ref, bg_ref, wu_ref, bu_ref, wd_ref, bd_ref, g2_ref, b2_ref,
                      y_ref, acc, *, alpha):
    e = pl.program_id(0)

    @pl.when(e == 0)
    def _():
        acc[...] = jnp.zeros(acc.shape, F32)

    x = x_ref[...]
    ti = ti_ref[...]
    tg = tg_ref[...]
    lane = lax.broadcasted_iota(I32, ti.shape, 1)
    cw = jnp.sum(jnp.where((ti == e) & (lane < TOP_K), tg, 0.0), axis=1, keepdims=True)
    out = _ffn(x.astype(BF16), wg_ref[...], bg_ref[...], wu_ref[...], bu_ref[...], wd_ref[...], bd_ref[...])
    acc[...] = acc[...] + cw * out

    @pl.when(e == pl.num_programs(0) - 1)
    def _():
        y_ref[...] = _layer_norm(alpha * x + acc[...], g2_ref[...], b2_ref[...])


def _moe_dense(x1, top_idx, top_gate, moe, layer, ln2_g, ln2_b, alpha):
    n = x1.shape[0]
    e = N_EXPERTS
    depth = moe["w_gate"].shape[0]
    b4 = lambda a: a.reshape(depth, e, 1, a.shape[-1])
    full = lambda a: pl.BlockSpec(a.shape, lambda i: (0,) * a.ndim)
    wspec = pl.BlockSpec((None, None, D_MODEL, D_FF), lambda i: (layer, i, 0, 0))
    wdspec = pl.BlockSpec((None, None, D_FF, D_MODEL), lambda i: (layer, i, 0, 0))
    bspec = pl.BlockSpec((None, None, 1, D_FF), lambda i: (layer, i, 0, 0))
    return pl.pallas_call(
        functools.partial(_moe_dense_kernel, alpha=alpha),
        grid=(e,),
        in_specs=[full(x1), full(top_idx), full(top_gate), wspec, bspec, wspec, bspec, wdspec, bspec,
                  full(ln2_g), full(ln2_b)],
        out_specs=pl.BlockSpec((n, D_MODEL), lambda i: (0, 0)),
        out_shape=jax.ShapeDtypeStruct((n, D_MODEL), F32),
        scratch_shapes=[pltpu.VMEM((n, D_MODEL), F32)],
        compiler_params=_cp(1),
        name="moe_dense",
    )(x1, top_idx, top_gate, moe["w_gate"], b4(moe["b_gate"]), moe["w_up"], b4(moe["b_up"]),
      moe["w_down"], b4(moe["b_down"]), ln2_g, ln2_b)


def _t5_bucket(rel):
    n = jnp.maximum(rel, 0)
    max_exact = N_BUCKETS // 2
    nf = jnp.maximum(n, 1).astype(F32)
    large = max_exact + (jnp.log(nf / max_exact) / math.log(MAX_DISTANCE / max_exact)
                         * (N_BUCKETS - max_exact)).astype(I32)
    large = jnp.minimum(large, N_BUCKETS - 1)
    return jnp.where(n < max_exact, n, large)


def _const_mats():
    hid = np.arange(RW_WIDTH) // RW_HEAD
    segm = (hid[:, None] == hid[None, :]).astype(np.float32)
    ltri = np.tril(np.ones((CHUNK, CHUNK), np.float32))
    return jnp.asarray(segm, BF16), jnp.asarray(ltri, BF16)


def _layer_weights(l, w_in, rw_mu, rw_w0, rw_wB, rw_a0, rw_aB, rw_gB, rw_kk, rw_ka, rw_rk, rw_gn_g, rw_gn_b,
                   mem_wk, mem_wv, w_br_rw, w_br_at, w_br_mem, w_out, ln1_g, ln1_b, ln2_g, ln2_b,
                   router_w, router_b, segm, ltri):
    wl = w_in[l]
    o_at = RW_IN
    o_mq = RW_IN + AT_IN
    o_g = o_mq + MEM_WIDTH
    w_at = wl[:, o_at:o_at + AT_IN]
    ikc = w_at[:, 4 * AT_WIDTH:4 * AT_WIDTH + IDX_HEAD]
    iwc = w_at[:, 4 * AT_WIDTH + IDX_HEAD:]
    w_at2 = jnp.concatenate([w_at[:, :4 * AT_WIDTH], ikc, ikc, iwc,
                             jnp.zeros((D_MODEL, LANES - IDX_HEADS), F32)], axis=1)
    row = lambda a: a[l].reshape(1, -1)
    zpad = jnp.zeros((RW_LORA_W, RW_WIDTH), F32)
    return {
        "w_rw": wl[:, :RW_IN].astype(BF16),
        "w_at": w_at2.astype(BF16),
        "w_mq": wl[:, o_mq:o_g].astype(BF16),
        "w_gate": wl[:, o_g:].astype(BF16),
        "w_mem": jnp.concatenate([mem_wk[l], mem_wv[l]], axis=1).astype(BF16),
        "mu": row(rw_mu), "w0": row(rw_w0), "a0": row(rw_a0), "kk": row(rw_kk), "ka": row(rw_ka),
        "rk": row(rw_rk), "gn_g": row(rw_gn_g), "gn_b": row(rw_gn_b),
        "wB2": jnp.concatenate([rw_wB[l], zpad], axis=0).astype(BF16),
        "aB2": jnp.concatenate([zpad, rw_aB[l]], axis=0).astype(BF16),
        "gB": rw_gB[l].astype(BF16),
        "segm": segm, "ltri": ltri,
        "w_br_rw": w_br_rw[l].astype(BF16), "w_br_at": w_br_at[l].astype(BF16),
        "w_br_mem": w_br_mem[l].astype(BF16), "w_out": w_out[l].astype(BF16),
        "ln1_g": row(ln1_g), "ln1_b": row(ln1_b), "ln2_g": row(ln2_g), "ln2_b": row(ln2_b),
        "router_w": jnp.concatenate([router_w[l], jnp.zeros((D_MODEL, LANES - N_EXPERTS), F32)], axis=1),
        "router_b": jnp.concatenate([router_b[l], jnp.full((LANES - N_EXPERTS,), NEG, F32)]).reshape(1, LANES),
    }


_AT_COLS = [(0, 512, AT_HEAD ** -0.5), (512, 512, 1.0), (1024, 512, 1.0), (512, 512, 1.0), (1024, 512, 1.0),
            (1536, 512, 1.0), (2048, 64, 1.0), (2048, 128, 1.0), (2176, 8, 1.0)]
_AT_DTYPES = [BF16, F32, F32, BF16, BF16, BF16, F32, BF16, F32]


def kernel(x_prompt, x_sample, mem_prompt, cache_k, cache_v, cache_idx_k, page_table, cache_mem_k, cache_mem_v, state_rwkv, state_shift, w_in, rw_mu, rw_w0, rw_wB, rw_a0, rw_aB, rw_gB, rw_kk, rw_ka, rw_rk, rw_gn_g, rw_gn_b, mem_wk, mem_wv, rel_bias, w_br_rw, w_br_at, w_br_mem, w_out, ln1_g, ln1_b, ln2_g, ln2_b, router_w, router_b, moe_w_gate, moe_b_gate, moe_w_up, moe_b_up, moe_w_down, moe_b_down):
    depth = w_in.shape[0]
    bp, seq, _ = x_prompt.shape
    db = x_sample.shape[0]
    n = bp * seq
    mlen = mem_prompt.shape[1]
    past = page_table.shape[1] * PAGE
    alpha = (2 * depth) ** 0.25
    segm, ltri = _const_mats()
    moe = {"w_gate": moe_w_gate, "b_gate": moe_b_gate, "w_up": moe_w_up, "b_up": moe_b_up,
           "w_down": moe_w_down, "b_down": moe_b_down}

    bias_d = rel_bias[_t5_bucket(jnp.arange(2 * QBLK))]
    far = rel_bias[N_BUCKETS - 1]
    ri = np.arange(QBLK)[:, None]
    ci = np.arange(LANES)[None, :]
    prev_tile = (bias_d[QBLK + ri - ci] - far).transpose(2, 0, 1)
    diag_tile = (bias_d[np.maximum(ri - ci, 0)] - far).transpose(2, 0, 1)
    bias3 = jnp.stack([jnp.zeros_like(prev_tile), prev_tile, diag_tile])
    near_s = rel_bias[_t5_bucket(past - jnp.arange(past - PAGE, past + 1))] - far
    btab = near_s[::-1].reshape(PAGE + 1, AT_HEADS, 1)

    xp = x_prompt.reshape(n, D_MODEL)
    xs = x_sample.reshape(db, D_MODEL)
    tm_rw = min(256, seq)
    acc = {k: [] for k in ("kp", "vp", "ikp", "mkp", "mvp", "sp", "shp", "ks", "vs", "iks", "ss", "shs")}

    for l in range(depth):
        lw = _layer_weights(l, w_in, rw_mu, rw_w0, rw_wB, rw_a0, rw_aB, rw_gB, rw_kk, rw_ka, rw_rk, rw_gn_g,
                            rw_gn_b, mem_wk, mem_wv, w_br_rw, w_br_at, w_br_mem, w_out, ln1_g, ln1_b,
                            ln2_g, ln2_b, router_w, router_b, segm, ltri)

        acc["shp"].append(xp.reshape(bp, seq, D_MODEL)[:, -1])
        mk, mv = _proj(mem_prompt.reshape(bp * mlen, D_MODEL), lw["w_mem"],
                       [(0, MEM_WIDTH, 1.0), (MEM_WIDTH, MEM_WIDTH, 1.0)], [F32, F32], 256)
        acc["mkp"].append(mk.reshape(bp, mlen, MEM_HEADS, MEM_HEAD))
        acc["mvp"].append(mv.reshape(bp, mlen, MEM_HEADS, MEM_HEAD))
        (p_rw,) = _proj(xp, lw["w_rw"], [(0, RW_IN, 1.0)], [F32], 512)
        q_bf, k32, v32, k_bf, v_bf, iq_bf, ik32, ik2_bf, iw = _proj(xp, lw["w_at"], _AT_COLS, _AT_DTYPES, 512)
        acc["kp"].append(k32.reshape(bp, seq, AT_HEADS, AT_HEAD))
        acc["vp"].append(v32.reshape(bp, seq, AT_HEADS, AT_HEAD))
        acc["ikp"].append(ik32.reshape(bp, seq, IDX_HEAD))

        y1, y2, bonus, g, ac, cc = _rw_chunks(p_rw, jnp.zeros((bp, 1, RW_IN), F32), lw, bp, seq, tm_rw)
        y_rw, hfin = _rw_scan(y1, y2, bonus, g, ac, cc, jnp.zeros((bp, LANES, RW_WIDTH), F32), lw, bp, seq, tm_rw)
        hf = hfin.reshape(bp, 2, RW_HEAD, RW_HEADS // 2, 2, RW_HEAD)
        s_fin = jnp.stack([hf[:, hh, :, :, hh, :] for hh in range(2)], axis=3)
        acc["sp"].append(s_fin.reshape(bp, RW_HEAD, RW_HEADS, RW_HEAD).transpose(0, 2, 3, 1))

        y_at = _dsa_prompt(iq_bf, iw, q_bf, ik2_bf, k_bf, v_bf, bias3, bp, seq)
        y_mem = _mem_attend(xp.reshape(bp, seq, D_MODEL), lw["w_mq"], mk.reshape(bp, mlen, MEM_WIDTH),
                            mv.reshape(bp, mlen, MEM_WIDTH), 512).reshape(n, MEM_WIDTH)
        x1, ti, tg = _merge(xp, y_rw, y_at, y_mem, lw, 256, alpha)
        xp = _moe_prompt(x1, ti[:, :TOP_K], tg[:, :TOP_K], moe, l, lw["ln2_g"], lw["ln2_b"], alpha, 256, 128)

        acc["shs"].append(xs)
        (p2,) = _proj(jnp.concatenate([xs, state_shift[l]], axis=0), lw["w_rw"], [(0, RW_IN, 1.0)], [F32], 2 * db)
        qs, ks32, vs32, _, _, iqs_bf, iks32, _, iws = _proj(xs, lw["w_at"], [(0, 512, 1.0)] + _AT_COLS[1:],
                                                            [F32] + _AT_DTYPES[1:], db)
        acc["ks"].append(ks32.reshape(db, 1, AT_HEADS, AT_HEAD))
        acc["vs"].append(vs32.reshape(db, 1, AT_HEADS, AT_HEAD))
        acc["iks"].append(iks32.reshape(db, 1, IDX_HEAD))
        r_, k_, v_, kk_, b_, w_, g_ = _rw_step_prep(p2[:db], p2[db:], lw)
        s_new, y_rw_s = _rw_step(state_rwkv[l], r_, w_, k_, kk_, b_, v_, g_, lw)
        acc["ss"].append(s_new)
        mb = _dsa_sel(page_table, iqs_bf, iws, iks32, cache_idx_k, l, past)
        sel_rows = _sel_rows(past)
        idx = _dsa_compact(mb.reshape(db, sel_rows, LANES), min(TOPK_MAX, (past + 1) // 4), past // PAGE + 1)
        y_at_s = _dsa_gather(page_table, idx.reshape(db, -1), qs, ks32, vs32, btab, cache_k, cache_v, l, past)
        y_mem_s = _mem_attend(xs.reshape(db, 1, D_MODEL), lw["w_mq"], cache_mem_k[l].reshape(db, mlen, MEM_WIDTH),
                              cache_mem_v[l].reshape(db, mlen, MEM_WIDTH), 1).reshape(db, MEM_WIDTH)
        x1s, tis, tgs = _merge(xs, y_rw_s.astype(BF16), y_at_s, y_mem_s, lw, db, alpha)
        xs = _moe_dense(x1s, tis, tgs, moe, l, lw["ln2_g"], lw["ln2_b"], alpha)

    st = lambda k: jnp.stack(acc[k])
    return (xp.reshape(bp, seq, D_MODEL), xs.reshape(db, 1, D_MODEL),
            st("kp"), st("vp"), st("ikp"), st("mkp"), st("mvp"), st("sp"), st("shp"),
            st("ks"), st("vs"), st("iks"), st("ss"), st("shs"))
```
